```python
import jax
import jax.numpy as jnp
from jax import lax
import numpy as np

D_MODEL = 2048
BATCH = 4
SEQ = 4096
DEPTH = 2
DEC_BATCH = 128
DEC_SEQ = 1
PAST_LEN = 16384
PAGE_SIZE = 128

N_A_LAYERS = DEPTH // 2
N_B_LAYERS = DEPTH - N_A_LAYERS
MLA_HEADS = 16
Q_LORA = 512
KV_LORA = 512
NOPE_DIM = 128
ROPE_DIM = 64
V_DIM = 128
SWA_GROUPS = ((128, 1), (512, 4), (2048, 16))
HEADS_PER_GROUP = 4
N_SWA_HEADS = HEADS_PER_GROUP * len(SWA_GROUPS)
SWA_HEAD_DIM = 128
ROT_DIM = SWA_HEAD_DIM // 4
ROPE_THETA = 500000.0
D_FF = 4 * D_MODEL
Q_BLOCK = 128
EPS = 1e-6
MLA_SCALE = (NOPE_DIM + ROPE_DIM) ** -0.5
SWA_SCALE = SWA_HEAD_DIM ** -0.5
N_PAGES = PAST_LEN // PAGE_SIZE
N_PHYS_PAGES = DEC_BATCH * N_PAGES + max(1, (DEC_BATCH * N_PAGES) // 4)

kernel_name = 'yoco_mla_dilated_swa_decoder_step'


def rms_norm(x, g):
    x32 = x.astype(jnp.float32)
    y = x32 * lax.rsqrt(jnp.mean(x32 * x32, axis=-1, keepdims=True) + EPS)
    return (y * g.astype(jnp.float32)).astype(x.dtype)


def rotary(x, pos, rot_dim):
    half = rot_dim // 2
    inv_freq = ROPE_THETA ** (-(jnp.arange(half, dtype=jnp.float32) * 2.0 / rot_dim))
    ang = pos.astype(jnp.float32)[:, None] * inv_freq[None, :]
    cos = jnp.cos(ang)[:, None, :]
    sin = jnp.sin(ang)[:, None, :]
    xr = x[..., :rot_dim].astype(jnp.float32)
    x1, x2 = xr[..., :half], xr[..., half:]
    rot = jnp.concatenate([x1 * cos - x2 * sin, x2 * cos + x1 * sin], axis=-1).astype(x.dtype)
    return jnp.concatenate([rot, x[..., rot_dim:]], axis=-1)


def ada_mod(c, w, b):
    return jax.nn.silu(c) @ w + b


def modulate(xn, shift, scale):
    return xn * (1.0 + scale[:, None, :]) + shift[:, None, :]


def sq_relu_mlp(xn, w1, w2):
    h = jax.nn.relu(xn @ w1)
    return (h * h) @ w2


def mla_project(xn, pos, p):
    a = xn @ p['w_in']
    cq = rms_norm(a[..., :Q_LORA], p['g_qa'])
    lat = rms_norm(a[..., Q_LORA:Q_LORA + KV_LORA], p['g_kva'])
    kr_raw = a[..., Q_LORA + KV_LORA:]
    q = jnp.einsum('btr,rhe->bthe', cq, p['w_uq'])
    q_nope = rms_norm(q[..., :NOPE_DIM], p['g_qn'])
    q_rope = rotary(rms_norm(q[..., NOPE_DIM:], p['g_qr']), pos, ROPE_DIM)
    k_rope = rotary(rms_norm(kr_raw, p['g_kr'])[:, :, None, :], pos, ROPE_DIM)[:, :, 0, :]
    return q_nope, q_rope, lat, k_rope


def mla_prompt_attention(q_nope, q_rope, lat, k_rope, p):
    b, t = lat.shape[:2]
    k_nope = rms_norm(jnp.einsum('btr,rhe->bthe', lat, p['w_uk']), p['g_kn'])
    v = jnp.einsum('btr,rhe->bthe', lat, p['w_uv'])
    nblk = t // Q_BLOCK
    qn_blocks = q_nope.reshape(b, nblk, Q_BLOCK, MLA_HEADS, NOPE_DIM).swapaxes(0, 1)
    qr_blocks = q_rope.reshape(b, nblk, Q_BLOCK, MLA_HEADS, ROPE_DIM).swapaxes(0, 1)
    k_pos = jnp.arange(t)

    def block(args):
        i, qn, qr = args
        s = (jnp.einsum('bqhe,bkhe->bhqk', qn, k_nope)
             + jnp.einsum('bqhe,bke->bhqk', qr, k_rope)).astype(jnp.float32) * MLA_SCALE
        q_pos = i * Q_BLOCK + jnp.arange(Q_BLOCK)
        s = jnp.where(k_pos[None, :] <= q_pos[:, None], s, -jnp.inf)
        pr = jax.nn.softmax(s, axis=-1).astype(v.dtype)
        return jnp.einsum('bhqk,bkhe->bqhe', pr, v)

    o = lax.map(block, (jnp.arange(nblk), qn_blocks, qr_blocks))
    o = o.swapaxes(0, 1).reshape(b, t, MLA_HEADS, V_DIM)
    return jnp.einsum('bthe,hed->btd', o, p['w_o'])


def mla_sample_attention(q_nope, q_rope, lat, k_rope, cache_lat, cache_kr, page_table, layer, p):
    def page_stats(phys):
        lat_pg = cache_lat[phys, :, layer]
        kr_pg = cache_kr[phys, :, layer]
        k_nope = rms_norm(jnp.einsum('bpr,rhe->bphe', lat_pg, p['w_uk']), p['g_kn'])
        s = (jnp.einsum('bshe,bphe->bhsp', q_nope, k_nope)
             + jnp.einsum('bshe,bpe->bhsp', q_rope, kr_pg)).astype(jnp.float32) * MLA_SCALE
        m = jnp.max(s, axis=-1)
        e = jnp.exp(s - m[..., None])
        return m, jnp.sum(e, axis=-1), jnp.einsum('bhsp,bpr->bhsr', e, lat_pg.astype(jnp.float32))

    m_pg, l_pg, acc_pg = lax.map(page_stats, page_table.T)
    s_len = lat.shape[1]
    k_nope_n = rms_norm(jnp.einsum('bsr,rhe->bshe', lat, p['w_uk']), p['g_kn'])
    s_n = (jnp.einsum('bshe,bthe->bhst', q_nope, k_nope_n)
           + jnp.einsum('bshe,bte->bhst', q_rope, k_rope)).astype(jnp.float32) * MLA_SCALE
    causal = jnp.arange(s_len)[None, :] <= jnp.arange(s_len)[:, None]
    s_n = jnp.where(causal, s_n, -jnp.inf)
    m_n = jnp.max(s_n, axis=-1)
    e_n = jnp.exp(s_n - m_n[..., None])
    l_n = jnp.sum(e_n, axis=-1)
    acc_n = jnp.einsum('bhst,btr->bhsr', e_n, lat.astype(jnp.float32))
    m_all = jnp.maximum(jnp.max(m_pg, axis=0), m_n)
    w_pg = jnp.exp(m_pg - m_all[None])
    w_n = jnp.exp(m_n - m_all)
    num = jnp.einsum('nbhs,nbhsr->bhsr', w_pg, acc_pg) + w_n[..., None] * acc_n
    den = jnp.sum(w_pg * l_pg, axis=0) + w_n * l_n
    o_lat = (num / den[..., None]).astype(lat.dtype)
    o = jnp.einsum('bhsr,rhe->bshe', o_lat, p['w_uv'])
    return jnp.einsum('bshe,hed->bsd', o, p['w_o'])


def shared_kv(h, c, pos, g_kv_norm, w_ada_kv, b_ada_kv, w_kv, g_swa_k):
    shift, scale = jnp.split(ada_mod(c, w_ada_kv, b_ada_kv), 2, axis=-1)
    hn = modulate(rms_norm(h, g_kv_norm), shift, scale)
    kv = jnp.einsum('btd,dcne->btcne', hn, w_kv)
    k = rotary(rms_norm(kv[:, :, 0], g_swa_k), pos, ROT_DIM)
    return k, kv[:, :, 1]


def swa_query(xn, pos, w_q, g_q):
    q = jnp.einsum('btd,dhe->bthe', xn, w_q)
    return rotary(rms_norm(q, g_q), pos, ROT_DIM)


def dilated_prompt_attention(q, k, v, window, dilation):
    b, t, h, e = q.shape
    band = window // dilation
    span = band * dilation
    t_pad = -(-t // span) * span
    m_len = t_pad // dilation
    nb = m_len // band

    def to_blocks(x):
        x = jnp.pad(x, ((0, 0), (0, t_pad - t), (0, 0), (0, 0)))
        x = x.reshape(b, m_len, dilation, h, e).transpose(0, 2, 1, 3, 4)
        return x.reshape(b, dilation, nb, band, h, e)

    def with_prev(x):
        prev = jnp.pad(x, ((0, 0), (0, 0), (1, 0), (0, 0), (0, 0), (0, 0)))[:, :, :-1]
        return jnp.concatenate([prev, x], axis=3)

    qb = to_blocks(q)
    kk = with_prev(to_blocks(k))
    vv = with_prev(to_blocks(v))
    s = jnp.einsum('brnqhe,brnkhe->brnhqk', qb, kk).astype(jnp.float32) * SWA_SCALE
    qi = jnp.arange(band)[:, None]
    ki = jnp.arange(2 * band)[None, :]
    lag = qi + band - ki
    band_mask = (lag >= 0) & (lag <= band)
    first_block_ok = (jnp.arange(nb)[:, None, None] > 0) | (ki[None] >= band)
    mask = band_mask[None] & first_block_ok
    s = jnp.where(mask[None, None, :, None], s, -jnp.inf)
    lse = jax.nn.logsumexp(s, axis=-1)
    pr = jnp.exp(s - lse[..., None]).astype(v.dtype)
    o = jnp.einsum('brnhqk,brnkhe->brnqhe', pr, vv)
    o = o.reshape(b, dilation, m_len, h, e).transpose(0, 2, 1, 3, 4).reshape(b, t_pad, h, e)[:, :t]
    lse = lse.transpose(0, 1, 2, 4, 3).reshape(b, dilation, m_len, h).transpose(0, 2, 1, 3).reshape(b, t_pad, h)[:, :t]
    return o, lse


def dilated_sample_attention(q, k_new, v_new, k_buf, v_buf, window, dilation):
    s_len = q.shape[1]
    buf_len = k_buf.shape[1]
    band = window // dilation
    rel = jnp.arange(s_len)[:, None] - dilation * jnp.arange(band + 1)[None, :]
    is_new = rel >= 0
    buf_idx = buf_len + rel
    valid = buf_idx >= 0
    new_i = jnp.clip(rel, 0, s_len - 1)
    buf_i = jnp.clip(buf_idx, 0, buf_len - 1)
    sel = is_new[None, :, :, None, None]
    k_sel = jnp.where(sel, k_new[:, new_i], k_buf[:, buf_i])
    v_sel = jnp.where(sel, v_new[:, new_i], v_buf[:, buf_i])
    s = jnp.einsum('bshe,bsnhe->bhsn', q, k_sel).astype(jnp.float32) * SWA_SCALE
    s = jnp.where(valid[None, None], s, -jnp.inf)
    lse = jax.nn.logsumexp(s, axis=-1)
    pr = jnp.exp(s - lse[..., None]).astype(v_new.dtype)
    o = jnp.einsum('bhsn,bsnhe->bshe', pr, v_sel)
    return o, lse.transpose(0, 2, 1)


def combine_groups(outs, lses, w_o):
    alpha = jax.nn.softmax(jnp.stack(lses, axis=0).astype(jnp.float32), axis=0)
    o = jnp.concatenate([o_g * alpha[g][..., None].astype(o_g.dtype) for g, o_g in enumerate(outs)], axis=2)
    return jnp.einsum('bthe,hed->btd', o, w_o)


def group_heads(x, g):
    return x[:, :, g * HEADS_PER_GROUP:(g + 1) * HEADS_PER_GROUP]


def window_tail(x, g):
    t = x.shape[1]
    return group_heads(x, g)[:, t - min(SWA_GROUPS[g][0], t):]


def setup_inputs(seed: int = 0) -> dict:
    key = jax.random.key(seed)
    keys = iter(jax.random.split(key, 64))

    def normal(shape, scale):
        return jax.random.normal(next(keys), shape, jnp.float32) * scale

    def gain(shape):
        return 1.0 + normal(shape, 0.05)

    def buf(g):
        return normal((DEC_BATCH, min(SWA_GROUPS[g][0], PAST_LEN), HEADS_PER_GROUP, SWA_HEAD_DIM), 1.0)

    page_table = jax.random.permutation(next(keys), N_PHYS_PAGES)[:DEC_BATCH * N_PAGES]
    page_table = page_table.reshape(DEC_BATCH, N_PAGES).astype(jnp.int32)
    d_in = D_MODEL ** -0.5
    return {
        'x_prompt': normal((BATCH, SEQ, D_MODEL), 1.0),
        'x_sample': normal((DEC_BATCH, DEC_SEQ, D_MODEL), 1.0),
        'c_prompt': normal((BATCH, D_MODEL), 1.0),
        'c_sample': normal((DEC_BATCH, D_MODEL), 1.0),
        'page_table': page_table,
        'cache_mla_latent': normal((N_PHYS_PAGES, PAGE_SIZE, N_A_LAYERS, KV_LORA), 1.0),
        'cache_mla_krope': normal((N_PHYS_PAGES, PAGE_SIZE, N_A_LAYERS, ROPE_DIM), 1.0),
        'state_swa_k0': buf(0),
        'state_swa_v0': buf(0),
        'state_swa_k1': buf(1),
        'state_swa_v1': buf(1),
        'state_swa_k2': buf(2),
        'state_swa_v2': buf(2),
        'g_norm_mix': gain((DEPTH, D_MODEL)),
        'g_norm_ff': gain((DEPTH, D_MODEL)),
        'w_ada': normal((DEPTH, D_MODEL, 6 * D_MODEL), 0.5 * d_in),
        'b_ada': normal((DEPTH, 6 * D_MODEL), 0.01),
        'w_ff1': normal((DEPTH, D_MODEL, D_FF), d_in),
        'w_ff2': normal((DEPTH, D_FF, D_MODEL), D_FF ** -0.5),
        'w_mla_in': normal((N_A_LAYERS, D_MODEL, Q_LORA + KV_LORA + ROPE_DIM), d_in),
        'g_mla_qa': gain((N_A_LAYERS, Q_LORA)),
        'g_mla_kva': gain((N_A_LAYERS, KV_LORA)),
        'w_mla_uq': normal((N_A_LAYERS, Q_LORA, MLA_HEADS, NOPE_DIM + ROPE_DIM), Q_LORA ** -0.5),
        'w_mla_uk': normal((N_A_LAYERS, KV_LORA, MLA_HEADS, NOPE_DIM), KV_LORA ** -0.5),
        'w_mla_uv': normal((N_A_LAYERS, KV_LORA, MLA_HEADS, V_DIM), KV_LORA ** -0.5),
        'g_mla_qn': gain((N_A_LAYERS, NOPE_DIM)),
        'g_mla_qr': gain((N_A_LAYERS, ROPE_DIM)),
        'g_mla_kn': gain((N_A_LAYERS, NOPE_DIM)),
        'g_mla_kr': gain((N_A_LAYERS, ROPE_DIM)),
        'w_mla_o': normal((N_A_LAYERS, MLA_HEADS, V_DIM, D_MODEL), (MLA_HEADS * V_DIM) ** -0.5),
        'g_kv_norm': gain((D_MODEL,)),
        'w_ada_kv': normal((D_MODEL, 2 * D_MODEL), 0.5 * d_in),
        'b_ada_kv': normal((2 * D_MODEL,), 0.01),
        'w_kv': normal((D_MODEL, 2, N_SWA_HEADS, SWA_HEAD_DIM), d_in),
        'g_swa_k': gain((SWA_HEAD_DIM,)),
        'w_swa_q': normal((N_B_LAYERS, D_MODEL, N_SWA_HEADS, SWA_HEAD_DIM), d_in),
        'g_swa_q': gain((N_B_LAYERS, SWA_HEAD_DIM)),
        'w_swa_o': normal((N_B_LAYERS, N_SWA_HEADS, SWA_HEAD_DIM, D_MODEL), (N_SWA_HEADS * SWA_HEAD_DIM) ** -0.5),
    }


def reference(x_prompt, x_sample, c_prompt, c_sample, page_table,
              cache_mla_latent, cache_mla_krope,
              state_swa_k0, state_swa_v0, state_swa_k1, state_swa_v1, state_swa_k2, state_swa_v2,
              g_norm_mix, g_norm_ff, w_ada, b_ada, w_ff1, w_ff2,
              w_mla_in, g_mla_qa, g_mla_kva, w_mla_uq, w_mla_uk, w_mla_uv,
              g_mla_qn, g_mla_qr, g_mla_kn, g_mla_kr, w_mla_o,
              g_kv_norm, w_ada_kv, b_ada_kv, w_kv, g_swa_k,
              w_swa_q, g_swa_q, w_swa_o):
    pos_p = jnp.arange(x_prompt.shape[1], dtype=jnp.float32)
    pos_s = PAST_LEN + jnp.arange(x_sample.shape[1], dtype=jnp.float32)
    swa_state = ((state_swa_k0, state_swa_v0), (state_swa_k1, state_swa_v1), (state_swa_k2, state_swa_v2))
    xp, xs = x_prompt, x_sample
    lat_p_rows, kr_p_rows, lat_s_rows, kr_s_rows = [], [], [], []
    k_p = v_p = k_s = v_s = None
    for l in range(DEPTH):
        mp = jnp.split(ada_mod(c_prompt, w_ada[l], b_ada[l]), 6, axis=-1)
        ms = jnp.split(ada_mod(c_sample, w_ada[l], b_ada[l]), 6, axis=-1)
        xn_p = modulate(rms_norm(xp, g_norm_mix[l]), mp[0], mp[1])
        xn_s = modulate(rms_norm(xs, g_norm_mix[l]), ms[0], ms[1])
        if l < N_A_LAYERS:
            pa = {'w_in': w_mla_in[l], 'g_qa': g_mla_qa[l], 'g_kva': g_mla_kva[l], 'w_uq': w_mla_uq[l],
                  'w_uk': w_mla_uk[l], 'w_uv': w_mla_uv[l], 'g_qn': g_mla_qn[l], 'g_qr': g_mla_qr[l],
                  'g_kn': g_mla_kn[l], 'g_kr': g_mla_kr[l], 'w_o': w_mla_o[l]}
            qn, qr, lat, kr = mla_project(xn_p, pos_p, pa)
            mix_p = mla_prompt_attention(qn, qr, lat, kr, pa)
            lat_p_rows.append(lat)
            kr_p_rows.append(kr)
            qn, qr, lat, kr = mla_project(xn_s, pos_s, pa)
            mix_s = mla_sample_attention(qn, qr, lat, kr, cache_mla_latent, cache_mla_krope, page_table, l, pa)
            lat_s_rows.append(lat)
            kr_s_rows.append(kr)
        else:
            if l == N_A_LAYERS:
                k_p, v_p = shared_kv(xp, c_prompt, pos_p, g_kv_norm, w_ada_kv, b_ada_kv, w_kv, g_swa_k)
                k_s, v_s = shared_kv(xs, c_sample, pos_s, g_kv_norm, w_ada_kv, b_ada_kv, w_kv, g_swa_k)
            bl = l - N_A_LAYERS
            q_p = swa_query(xn_p, pos_p, w_swa_q[bl], g_swa_q[bl])
            q_s = swa_query(xn_s, pos_s, w_swa_q[bl], g_swa_q[bl])
            outs_p, lses_p, outs_s, lses_s = [], [], [], []
            for g, (win, dil) in enumerate(SWA_GROUPS):
                o, lse = dilated_prompt_attention(group_heads(q_p, g), group_heads(k_p, g), group_heads(v_p, g), win, dil)
                outs_p.append(o)
                lses_p.append(lse)
                o, lse = dilated_sample_attention(group_heads(q_s, g), group_heads(k_s, g), group_heads(v_s, g),
                                                  swa_state[g][0], swa_state[g][1], win, dil)
                outs_s.append(o)
                lses_s.append(lse)
            mix_p = combine_groups(outs_p, lses_p, w_swa_o[bl])
            mix_s = combine_groups(outs_s, lses_s, w_swa_o[bl])
        xp = xp + mp[2][:, None, :] * mix_p
        xs = xs + ms[2][:, None, :] * mix_s
        xp = xp + mp[5][:, None, :] * sq_relu_mlp(modulate(rms_norm(xp, g_norm_ff[l]), mp[3], mp[4]), w_ff1[l], w_ff2[l])
        xs = xs + ms[5][:, None, :] * sq_relu_mlp(modulate(rms_norm(xs, g_norm_ff[l]), ms[3], ms[4]), w_ff1[l], w_ff2[l])
    lat_prompt = jnp.stack(lat_p_rows, axis=2)
    krope_prompt = jnp.stack(kr_p_rows, axis=2)
    lat_sample = jnp.stack(lat_s_rows, axis=2)
    krope_sample = jnp.stack(kr_s_rows, axis=2)
    k0_prompt, v0_prompt = window_tail(k_p, 0), window_tail(v_p, 0)
    k1_prompt, v1_prompt = window_tail(k_p, 1), window_tail(v_p, 1)
    k2_prompt, v2_prompt = window_tail(k_p, 2), window_tail(v_p, 2)
    k0_sample, v0_sample = group_heads(k_s, 0), group_heads(v_s, 0)
    k1_sample, v1_sample = group_heads(k_s, 1), group_heads(v_s, 1)
    k2_sample, v2_sample = group_heads(k_s, 2), group_heads(v_s, 2)
    return (xp, xs, lat_prompt, krope_prompt, lat_sample, krope_sample,
            k0_prompt, v0_prompt, k1_prompt, v1_prompt, k2_prompt, v2_prompt,
            k0_sample, v0_sample, k1_sample, v1_sample, k2_sample, v2_sample)
```

```python
import functools

import jax
import jax.numpy as jnp
from jax import lax
from jax.experimental import pallas as pl
from jax.experimental.pallas import tpu as pltpu

F32 = jnp.float32
BF16 = jnp.bfloat16

MLA_HEADS = 16
Q_LORA = 512
KV_LORA = 512
NOPE_DIM = 128
ROPE_DIM = 64
V_DIM = 128
SWA_GROUPS = ((128, 1), (512, 4), (2048, 16))
HEADS_PER_GROUP = 4
N_SWA_HEADS = HEADS_PER_GROUP * len(SWA_GROUPS)
SWA_HEAD_DIM = 128
ROT_DIM = SWA_HEAD_DIM // 4
ROPE_THETA = 500000.0
PAST_LEN = 16384
EPS = 1e-6
MLA_SCALE = (NOPE_DIM + ROPE_DIM) ** -0.5
SWA_SCALE = SWA_HEAD_DIM ** -0.5

LANES = 128
SUBLANES = 8
VMEM_LIMIT_BYTES = 56 * 2**20
SWA_BAND = 128
GROUP_W = HEADS_PER_GROUP * SWA_HEAD_DIM
SWA_W = N_SWA_HEADS * SWA_HEAD_DIM
MLA_W = MLA_HEADS * LANES
MLA_QK = 2 * LANES
NEG_INF = float("-inf")


def _params(*sem):
    return pltpu.CompilerParams(dimension_semantics=sem, vmem_limit_bytes=VMEM_LIMIT_BYTES)


def _nt_dot(a, b):
    return lax.dot_general(a, b, (((1,), (1,)), ((), ())), preferred_element_type=F32)


def _dot(a, b):
    return jnp.dot(a, b, preferred_element_type=F32)


def _rms(x, g, n):
    ms = jnp.sum(x * x, axis=-1, keepdims=True) * (1.0 / n)
    return x * lax.rsqrt(ms + EPS) * g


def _rope(x, cos, sin_lo, sin_hi, half):
    return x * cos + pltpu.roll(x, LANES - half, 1) * sin_lo + pltpu.roll(x, half, 1) * sin_hi


def _norm_mod(x, g, shift, scale):
    return _rms(x, g, x.shape[-1]) * (1.0 + scale) + shift


def _rope_tables(pos, rot_dim):
    half = rot_dim // 2
    inv_freq = ROPE_THETA ** (-(jnp.arange(half, dtype=F32) * 2.0 / rot_dim))
    ang = pos.astype(F32)[:, None] * inv_freq[None, :]
    cos, sin = jnp.cos(ang), jnp.sin(ang)
    t = pos.shape[0]
    zeros = lambda n: jnp.zeros((t, n), F32)
    cos_t = jnp.concatenate([cos, cos, jnp.ones((t, LANES - rot_dim), F32)], axis=1)
    sin_lo = jnp.concatenate([-sin, zeros(LANES - half)], axis=1)
    sin_hi = jnp.concatenate([zeros(half), sin, zeros(LANES - rot_dim)], axis=1)
    return cos_t, sin_lo, sin_hi


class _Rows:
    def __init__(self, nb, t, tm, mod_view, mod_block, mod_index):
        self.nb, self.t, self.tm = nb, t, tm
        self._mod_view, self._mod_block, self._mod_index = mod_view, mod_block, mod_index

    def mod_arr(self, mod):
        return mod.reshape(self._mod_view(mod.shape[1]))

    def mod_spec(self, k, d):
        idx = self._mod_index
        return pl.BlockSpec(self._mod_block(d), lambda b, *_: idx(b, k))

    def row_spec(self, width):
        return pl.BlockSpec((1, self.tm, width), lambda b, i, *_: (b, i, 0))

    def tab_spec(self):
        return pl.BlockSpec((self.tm, LANES), lambda b, i, *_: (i, 0))


def _full(shape):
    return pl.BlockSpec(shape, lambda *_: (0,) * len(shape))


def _ada_kernel(c_ref, w_ref, b_ref, o_ref):
    c = c_ref[...]
    s = c / (1.0 + jnp.exp(-c))
    o_ref[...] = _dot(s.astype(BF16), w_ref[...].astype(BF16)) + b_ref[...]


def _ada(c, w, b, tn=1024):
    m, d = c.shape
    n = w.shape[1]
    return pl.pallas_call(
        _ada_kernel,
        grid=(n // tn,),
        in_specs=[pl.BlockSpec((m, d), lambda j: (0, 0)),
                  pl.BlockSpec((d, tn), lambda j: (0, j)),
                  pl.BlockSpec((1, tn), lambda j: (0, j))],
        out_specs=pl.BlockSpec((m, tn), lambda j: (0, j)),
        out_shape=jax.ShapeDtypeStruct((m, n), F32),
        compiler_params=_params("arbitrary"),
        name="ada_mod",
    )(c, w, b.reshape(1, n))


def _mla_in_kernel(x_ref, g_ref, sh_ref, sc_ref, w_ref, gqa_ref, gkva_ref, gkr_ref,
                   cos_ref, slo_ref, shi_ref, cq_ref, lat_ref, latb_ref, kr_ref, krb_ref):
    xn = _norm_mod(x_ref[0], g_ref[...], sh_ref[0], sc_ref[0])
    a = _dot(xn.astype(BF16), w_ref[...])
    cq_ref[0] = _rms(a[:, :Q_LORA], gqa_ref[...], Q_LORA).astype(BF16)
    lat = _rms(a[:, Q_LORA:Q_LORA + KV_LORA], gkva_ref[...], KV_LORA)
    lat_ref[0] = lat
    latb_ref[0] = lat.astype(BF16)
    kr = _rms(a[:, Q_LORA + KV_LORA:], gkr_ref[...], ROPE_DIM)
    kr = _rope(kr, cos_ref[...], slo_ref[...], shi_ref[...], ROPE_DIM // 2)
    kr_ref[0] = kr[:, :ROPE_DIM]
    krb_ref[0] = kr.astype(BF16)


def _mla_in(rows, x, g, mod, w, gqa, gkva, gkr, rope):
    d = x.shape[-1]
    n = w.shape[1]
    marr = rows.mod_arr(mod)
    shape = lambda width, dt: jax.ShapeDtypeStruct((rows.nb, rows.t, width), dt)
    return pl.pallas_call(
        _mla_in_kernel,
        grid=(rows.nb, rows.t // rows.tm),
        in_specs=[rows.row_spec(d), _full((1, d)), rows.mod_spec(0, d), rows.mod_spec(1, d),
                  _full((d, n)), _full((1, Q_LORA)), _full((1, KV_LORA)), _full((1, LANES)),
                  rows.tab_spec(), rows.tab_spec(), rows.tab_spec()],
        out_specs=[rows.row_spec(Q_LORA), rows.row_spec(KV_LORA), rows.row_spec(KV_LORA),
                   rows.row_spec(ROPE_DIM), rows.row_spec(LANES)],
        out_shape=[shape(Q_LORA, BF16), shape(KV_LORA, F32), shape(KV_LORA, BF16),
                   shape(ROPE_DIM, F32), shape(LANES, BF16)],
        compiler_params=_params("arbitrary", "arbitrary"),
        name="mla_in",
    )(x, g, marr, marr, w, gqa, gkva, gkr, *rope)


def _q_up_kernel(cq_ref, wn_ref, wr_ref, gqn_ref, gqr_ref, cos_ref, slo_ref, shi_ref, q_ref):
    cq = cq_ref[0]
    an = _dot(cq, wn_ref[...])
    ar = _dot(cq, wr_ref[...])
    cos, slo, shi = cos_ref[...], slo_ref[...], shi_ref[...]
    for h in range(MLA_HEADS):
        sl = slice(h * LANES, (h + 1) * LANES)
        qn = _rms(an[:, sl], gqn_ref[...], NOPE_DIM) * MLA_SCALE
        qr = _rope(_rms(ar[:, sl], gqr_ref[...], ROPE_DIM), cos, slo, shi, ROPE_DIM // 2) * MLA_SCALE
        q_ref[0, h, :, 0:LANES] = qn.astype(BF16)
        q_ref[0, h, :, LANES:MLA_QK] = qr.astype(BF16)


def _q_up(rows, cq, wn, wr, gqn, gqr, rope):
    tm = rows.tm
    return pl.pallas_call(
        _q_up_kernel,
        grid=(rows.nb, rows.t // tm),
        in_specs=[rows.row_spec(Q_LORA), _full((Q_LORA, MLA_W)), _full((Q_LORA, MLA_W)),
                  _full((1, LANES)), _full((1, LANES)),
                  rows.tab_spec(), rows.tab_spec(), rows.tab_spec()],
        out_specs=pl.BlockSpec((1, MLA_HEADS, tm, MLA_QK), lambda b, i: (b, 0, i, 0)),
        out_shape=jax.ShapeDtypeStruct((rows.nb, MLA_HEADS, rows.t, MLA_QK), BF16),
        compiler_params=_params("arbitrary", "arbitrary"),
        name="mla_q_up",
    )(cq, wn, wr, gqn, gqr, *rope)


def _kv_up_kernel(latb_ref, krb_ref, wk_ref, wv_ref, gkn_ref, k_ref, v_ref):
    lat = latb_ref[0]
    ak = _dot(lat, wk_ref[...])
    av = _dot(lat, wv_ref[...])
    krb = krb_ref[0]
    for h in range(MLA_HEADS):
        sl = slice(h * LANES, (h + 1) * LANES)
        k_ref[0, h, :, 0:LANES] = _rms(ak[:, sl], gkn_ref[...], NOPE_DIM).astype(BF16)
        k_ref[0, h, :, LANES:MLA_QK] = krb
        v_ref[0, h] = av[:, sl].astype(BF16)


def _kv_up(rows, latb, krb, wk, wv, gkn):
    tm = rows.tm
    hspec = lambda w: pl.BlockSpec((1, MLA_HEADS, tm, w), lambda b, i: (b, 0, i, 0))
    return pl.pallas_call(
        _kv_up_kernel,
        grid=(rows.nb, rows.t // tm),
        in_specs=[rows.row_spec(KV_LORA), rows.row_spec(LANES), _full((KV_LORA, MLA_W)),
                  _full((KV_LORA, MLA_W)), _full((1, LANES))],
        out_specs=[hspec(MLA_QK), hspec(V_DIM)],
        out_shape=[jax.ShapeDtypeStruct((rows.nb, MLA_HEADS, rows.t, MLA_QK), BF16),
                   jax.ShapeDtypeStruct((rows.nb, MLA_HEADS, rows.t, V_DIM), BF16)],
        compiler_params=_params("arbitrary", "arbitrary"),
        name="mla_kv_up",
    )(latb, krb, wk, wv, gkn)


def _flash_kernel(qi_ref, ki_ref, q_ref, k_ref, v_ref, o_ref, m_ref, l_ref, acc_ref):
    s_idx = pl.program_id(2)
    qi, ki = qi_ref[s_idx], ki_ref[s_idx]

    @pl.when(ki == 0)
    def _():
        m_ref[...] = jnp.full(m_ref.shape, NEG_INF, F32)
        l_ref[...] = jnp.zeros(l_ref.shape, F32)
        acc_ref[...] = jnp.zeros(acc_ref.shape, F32)

    def update(diagonal):
        s = _nt_dot(q_ref[0, 0], k_ref[0, 0])
        if diagonal:
            rows = lax.broadcasted_iota(jnp.int32, s.shape, 0)
            cols = lax.broadcasted_iota(jnp.int32, s.shape, 1)
            s = jnp.where(cols <= rows, s, NEG_INF)
        m_prev = m_ref[...]
        m_new = jnp.maximum(m_prev, jnp.max(s, axis=-1, keepdims=True))
        corr = jnp.exp(m_prev - m_new)
        p = jnp.exp(s - m_new)
        l_ref[...] = corr * l_ref[...] + jnp.sum(p, axis=-1, keepdims=True)
        acc_ref[...] = corr * acc_ref[...] + _dot(p.astype(BF16), v_ref[0, 0])
        m_ref[...] = m_new

    @pl.when(ki < qi)
    def _():
        update(False)

    @pl.when(ki == qi)
    def _():
        update(True)
        o_ref[0] = (acc_ref[...] / l_ref[...]).astype(o_ref.dtype)


def _flash(q, k, v, tq):
    nb, nh, t, _ = q.shape
    nq = t // tq
    pairs = [(i, j) for i in range(nq) for j in range(i + 1)]
    qi = jnp.asarray([p[0] for p in pairs], jnp.int32)
    ki = jnp.asarray([p[1] for p in pairs], jnp.int32)
    grid_spec = pltpu.PrefetchScalarGridSpec(
        num_scalar_prefetch=2,
        grid=(nb, nh, len(pairs)),
        in_specs=[pl.BlockSpec((1, 1, tq, MLA_QK), lambda b, h, s, qi, ki: (b, h, qi[s], 0)),
                  pl.BlockSpec((1, 1, tq, MLA_QK), lambda b, h, s, qi, ki: (b, h, ki[s], 0)),
                  pl.BlockSpec((1, 1, tq, V_DIM), lambda b, h, s, qi, ki: (b, h, ki[s], 0))],
        out_specs=pl.BlockSpec((1, tq, V_DIM), lambda b, h, s, qi, ki: (b, qi[s], h)),
        scratch_shapes=[pltpu.VMEM((tq, 1), F32), pltpu.VMEM((tq, 1), F32),
                        pltpu.VMEM((tq, V_DIM), F32)],
    )
    return pl.pallas_call(
        _flash_kernel,
        grid_spec=grid_spec,
        out_shape=jax.ShapeDtypeStruct((nb, t, nh * V_DIM), BF16),
        compiler_params=_params("arbitrary", "arbitrary", "arbitrary"),
        name="mla_flash",
    )(qi, ki, q, k, v)


def _proj_res_kernel(a_ref, w_ref, x_ref, gate_ref, o_ref):
    o_ref[0] = x_ref[0] + gate_ref[0] * _dot(a_ref[0], w_ref[...])


def _proj_res(rows, a, w, x, mod, gate_k):
    kdim, d = w.shape
    return pl.pallas_call(
        _proj_res_kernel,
        grid=(rows.nb, rows.t // rows.tm),
        in_specs=[rows.row_spec(kdim), _full((kdim, d)), rows.row_spec(d), rows.mod_spec(gate_k, d)],
        out_specs=rows.row_spec(d),
        out_shape=jax.ShapeDtypeStruct((rows.nb, rows.t, d), F32),
        compiler_params=_params("arbitrary", "arbitrary"),
        name="proj_residual",
    )(a, w, x, rows.mod_arr(mod))


def _group_softmax(lse):
    parts = [lse[:, g * GROUP_W:(g + 1) * GROUP_W] for g in range(len(SWA_GROUPS))]
    mx = jnp.maximum(jnp.maximum(parts[0], parts[1]), parts[2])
    es = [jnp.exp(p - mx) for p in parts]
    inv = 1.0 / (es[0] + es[1] + es[2])
    return [e * inv for e in es]


def _swa_proj_res_kernel(a_ref, lse_ref, w_ref, x_ref, gate_ref, o_ref):
    alphas = _group_softmax(lse_ref[0])
    a = a_ref[0]
    mixed = [(a[:, g * GROUP_W:(g + 1) * GROUP_W].astype(F32) * alphas[g]).astype(BF16)
             for g in range(len(SWA_GROUPS))]
    o_ref[0] = x_ref[0] + gate_ref[0] * _dot(jnp.concatenate(mixed, axis=1), w_ref[...])


def _swa_proj_res(rows, a, lse, w, x, mod, gate_k):
    kdim, d = w.shape
    return pl.pallas_call(
        _swa_proj_res_kernel,
        grid=(rows.nb, rows.t // rows.tm),
        in_specs=[rows.row_spec(kdim), rows.row_spec(kdim), _full((kdim, d)), rows.row_spec(d),
                  rows.mod_spec(gate_k, d)],
        out_specs=rows.row_spec(d),
        out_shape=jax.ShapeDtypeStruct((rows.nb, rows.t, d), F32),
        compiler_params=_params("arbitrary", "arbitrary"),
        name="swa_proj_residual",
    )(a, lse, w, x, rows.mod_arr(mod))


def _mlp_kernel(x_ref, g_ref, sh_ref, sc_ref, gate_ref, w1_ref, w2_ref, o_ref, xn_ref, acc_ref):
    f = pl.program_id(2)

    @pl.when(f == 0)
    def _():
        xn_ref[...] = _norm_mod(x_ref[0], g_ref[...], sh_ref[0], sc_ref[0]).astype(BF16)
        acc_ref[...] = jnp.zeros(acc_ref.shape, F32)

    h = jnp.maximum(_dot(xn_ref[...], w1_ref[...]), 0.0)
    acc_ref[...] += _dot((h * h).astype(BF16), w2_ref[...])

    @pl.when(f == pl.num_programs(2) - 1)
    def _():
        o_ref[0] = x_ref[0] + gate_ref[0] * acc_ref[...]


def _mlp(rows, x, g, mod, w1, w2, tf):
    d, dff = w1.shape
    tm = rows.tm
    marr = rows.mod_arr(mod)
    return pl.pallas_call(
        _mlp_kernel,
        grid=(rows.nb, rows.t // tm, dff // tf),
        in_specs=[rows.row_spec(d), _full((1, d)), rows.mod_spec(3, d), rows.mod_spec(4, d),
                  rows.mod_spec(5, d),
                  pl.BlockSpec((d, tf), lambda b, i, f: (0, f)),
                  pl.BlockSpec((tf, d), lambda b, i, f: (f, 0))],
        out_specs=rows.row_spec(d),
        out_shape=jax.ShapeDtypeStruct((rows.nb, rows.t, d), F32),
        scratch_shapes=[pltpu.VMEM((tm, d), BF16), pltpu.VMEM((tm, d), F32)],
        compiler_params=_params("arbitrary", "arbitrary", "arbitrary"),
        name="mlp",
    )(x, g, marr, marr, marr, w1, w2)


def _decode_kernel(pp, pt_ref, *refs):
    lat_refs, kr_refs = refs[:pp], refs[pp:2 * pp]
    (wukt_ref, wq_ref, q_ref, kc_ref, latn_ref, o_ref, m_ref, l_ref, acc_ref) = refs[2 * pp:]
    j = pl.program_id(1)

    @pl.when(j == 0)
    def _():
        m_ref[...] = jnp.full(m_ref.shape, NEG_INF, F32)
        l_ref[...] = jnp.zeros(l_ref.shape, F32)
        acc_ref[...] = jnp.zeros(acc_ref.shape, F32)

    wq = wq_ref[0]
    qr = q_ref[0][:, LANES:LANES + ROPE_DIM].astype(BF16)
    for i in range(0, pp, 2):
        lat = jnp.concatenate([lat_refs[i][0], lat_refs[i + 1][0]], axis=0).astype(BF16)
        kr = jnp.concatenate([kr_refs[i][0], kr_refs[i + 1][0]], axis=0).astype(BF16)
        raw_t = _nt_dot(wukt_ref[...], lat)
        raw3 = raw_t.reshape(MLA_HEADS, NOPE_DIM, raw_t.shape[-1])
        ssq = jnp.sum(raw3 * raw3, axis=1)
        s = _nt_dot(wq, lat) * lax.rsqrt(ssq * (1.0 / NOPE_DIM) + EPS) + _nt_dot(qr, kr)
        m_prev = m_ref[...]
        m_new = jnp.maximum(m_prev, jnp.max(s, axis=-1, keepdims=True))
        corr = jnp.exp(m_prev - m_new)
        p = jnp.exp(s - m_new)
        l_ref[...] = corr * l_ref[...] + jnp.sum(p, axis=-1, keepdims=True)
        acc_ref[...] = corr * acc_ref[...] + _dot(p.astype(BF16), lat)
        m_ref[...] = m_new

    @pl.when(j == pl.num_programs(1) - 1)
    def _():
        s_n = jnp.sum(q_ref[0] * kc_ref[0], axis=-1, keepdims=True)
        m_prev = m_ref[...]
        m_new = jnp.maximum(m_prev, s_n)
        corr = jnp.exp(m_prev - m_new)
        p_n = jnp.exp(s_n - m_new)
        l_fin = corr * l_ref[...] + p_n
        acc = corr * acc_ref[...] + p_n * latn_ref[0]
        o_ref[0] = acc / l_fin


def _decode(page_table, cache_lat, cache_kr, wukt, wq, q, kc, latn, pp):
    db, n_pages = page_table.shape
    psz = cache_lat.shape[1]
    pt = page_table.reshape(-1)

    def page_spec(width, i):
        return pl.BlockSpec((1, psz, width), lambda b, j, pt: (pt[b * n_pages + j * pp + i], 0, 0))

    per_b = lambda shape: pl.BlockSpec((1,) + shape, lambda b, j, pt: (b, 0, 0))
    grid_spec = pltpu.PrefetchScalarGridSpec(
        num_scalar_prefetch=1,
        grid=(db, n_pages // pp),
        in_specs=([page_spec(KV_LORA, i) for i in range(pp)] + [page_spec(ROPE_DIM, i) for i in range(pp)]
                  + [pl.BlockSpec(wukt.shape, lambda b, j, pt: (0, 0)),
                     per_b((MLA_HEADS, KV_LORA)), per_b((MLA_HEADS, MLA_QK)), per_b((MLA_HEADS, MLA_QK)),
                     per_b((1, KV_LORA))]),
        out_specs=per_b((MLA_HEADS, KV_LORA)),
        scratch_shapes=[pltpu.VMEM((MLA_HEADS, 1), F32), pltpu.VMEM((MLA_HEADS, 1), F32),
                        pltpu.VMEM((MLA_HEADS, KV_LORA), F32)],
    )
    return pl.pallas_call(
        functools.partial(_decode_kernel, pp),
        grid_spec=grid_spec,
        out_shape=jax.ShapeDtypeStruct((db, MLA_HEADS, KV_LORA), F32),
        compiler_params=_params("arbitrary", "arbitrary"),
        name="mla_decode",
    )(pt, *([cache_lat] * pp), *([cache_kr] * pp), wukt, wq, q, kc, latn)


def _absorb_kernel(q_ref, gkn_ref, wukt_ref, o_ref):
    qg = (q_ref[0, 0][:, :LANES].astype(F32) * gkn_ref[...]).astype(BF16)
    o_ref[0] = _dot(qg, wukt_ref[0]).astype(BF16)


def _absorb(q, gkn, wukt3):
    _, nh, db, _ = q.shape
    return pl.pallas_call(
        _absorb_kernel,
        grid=(nh,),
        in_specs=[pl.BlockSpec((1, 1, db, MLA_QK), lambda h: (0, h, 0, 0)), _full((1, LANES)),
                  pl.BlockSpec((1, NOPE_DIM, KV_LORA), lambda h: (h, 0, 0))],
        out_specs=pl.BlockSpec((1, db, KV_LORA), lambda h: (h, 0, 0)),
        out_shape=jax.ShapeDtypeStruct((nh, db, KV_LORA), BF16),
        compiler_params=_params("arbitrary"),
        name="mla_absorb_q",
    )(q, gkn, wukt3)


def _uv_kernel(o_ref, w_ref, out_ref):
    out_ref[...] = _dot(o_ref[0].astype(BF16), w_ref[0]).astype(BF16)


def _uv(o_lat, wuv3):
    nh, db, _ = o_lat.shape
    return pl.pallas_call(
        _uv_kernel,
        grid=(nh,),
        in_specs=[pl.BlockSpec((1, db, KV_LORA), lambda h: (h, 0, 0)),
                  pl.BlockSpec((1, KV_LORA, V_DIM), lambda h: (h, 0, 0))],
        out_specs=pl.BlockSpec((db, V_DIM), lambda h: (0, h)),
        out_shape=jax.ShapeDtypeStruct((db, nh * V_DIM), BF16),
        compiler_params=_params("arbitrary"),
        name="mla_uv",
    )(o_lat, wuv3)


def _head_norm_rope(a, g, cos, slo, shi, scale):
    outs = []
    for h in range(a.shape[-1] // LANES):
        y = _rope(_rms(a[:, h * LANES:(h + 1) * LANES], g, SWA_HEAD_DIM), cos, slo, shi, ROT_DIM // 2)
        outs.append(y * scale if scale != 1.0 else y)
    return jnp.concatenate(outs, axis=1)


def _swa_q_kernel(x_ref, g_ref, sh_ref, sc_ref, w_ref, gq_ref, cos_ref, slo_ref, shi_ref, q_ref):
    xn = _norm_mod(x_ref[0], g_ref[...], sh_ref[0], sc_ref[0])
    a = _dot(xn.astype(BF16), w_ref[...])
    q = _head_norm_rope(a, gq_ref[...], cos_ref[...], slo_ref[...], shi_ref[...], SWA_SCALE)
    q_ref[0] = q.astype(BF16)


def _swa_q(rows, x, g, mod, w, gq, rope):
    d, n = w.shape
    marr = rows.mod_arr(mod)
    return pl.pallas_call(
        _swa_q_kernel,
        grid=(rows.nb, rows.t // rows.tm),
        in_specs=[rows.row_spec(d), _full((1, d)), rows.mod_spec(0, d), rows.mod_spec(1, d),
                  _full((d, n)), _full((1, LANES)), rows.tab_spec(), rows.tab_spec(), rows.tab_spec()],
        out_specs=rows.row_spec(n),
        out_shape=jax.ShapeDtypeStruct((rows.nb, rows.t, n), BF16),
        compiler_params=_params("arbitrary", "arbitrary"),
        name="swa_q",
    )(x, g, marr, marr, w, gq, *rope)


def _shared_kv_kernel(x_ref, g_ref, sh_ref, sc_ref, w_ref, gk_ref, cos_ref, slo_ref, shi_ref,
                      k_ref, kb_ref, v_ref, vb_ref, xn_ref):
    c = pl.program_id(2)

    @pl.when(c == 0)
    def _():
        xn_ref[...] = _norm_mod(x_ref[0], g_ref[...], sh_ref[0], sc_ref[0]).astype(BF16)
        a = _dot(xn_ref[...], w_ref[...])
        k = _head_norm_rope(a, gk_ref[...], cos_ref[...], slo_ref[...], shi_ref[...], 1.0)
        k_ref[0] = k
        kb_ref[0] = k.astype(BF16)

    @pl.when(c == 1)
    def _():
        v = _dot(xn_ref[...], w_ref[...])
        v_ref[0] = v
        vb_ref[0] = v.astype(BF16)


def _shared_kv(rows, x, g, mod, w, gk, rope):
    d = x.shape[-1]
    marr = rows.mod_arr(mod)
    shape = lambda dt: jax.ShapeDtypeStruct((rows.nb, rows.t, SWA_W), dt)
    return pl.pallas_call(
        _shared_kv_kernel,
        grid=(rows.nb, rows.t // rows.tm, 2),
        in_specs=[rows.row_spec(d), _full((1, d)), rows.mod_spec(0, d), rows.mod_spec(1, d),
                  pl.BlockSpec((d, SWA_W), lambda b, i, c: (0, c)), _full((1, LANES)),
                  rows.tab_spec(), rows.tab_spec(), rows.tab_spec()],
        out_specs=[rows.row_spec(SWA_W)] * 4,
        out_shape=[shape(F32), shape(BF16), shape(F32), shape(BF16)],
        scratch_shapes=[pltpu.VMEM((rows.tm, d), BF16)],
        compiler_params=_params("arbitrary", "arbitrary", "arbitrary"),
        name="shared_kv",
    )(x, g, marr, marr, w, gk, *rope)


def _swa_prompt_kernel(q_ref, kp_ref, kc_ref, vp_ref, vc_ref, o_ref, lse_ref):
    n = pl.program_id(1)
    band = SWA_BAND
    qi = lax.broadcasted_iota(jnp.int32, (band, 2 * band), 0)
    ki = lax.broadcasted_iota(jnp.int32, (band, 2 * band), 1)
    lag = qi + band - ki
    first_key = jnp.where(n > 0, 0, band)
    ok = (lag >= 0) & (lag <= band) & (ki >= first_key)
    for h in range(HEADS_PER_GROUP):
        sl = slice(h * SWA_HEAD_DIM, (h + 1) * SWA_HEAD_DIM)
        k = jnp.concatenate([kp_ref[0][:, sl], kc_ref[0][:, sl]], axis=0)
        v = jnp.concatenate([vp_ref[0][:, sl], vc_ref[0][:, sl]], axis=0)
        s = jnp.where(ok, _nt_dot(q_ref[0][:, sl], k), NEG_INF)
        m = jnp.max(s, axis=-1, keepdims=True)
        p = jnp.exp(s - m)
        l = jnp.sum(p, axis=-1, keepdims=True)
        o_ref[0, :, sl] = (_dot(p.astype(BF16), v) / l).astype(BF16)
        lse_ref[0, :, sl] = jnp.broadcast_to(m + jnp.log(l), (band, SWA_HEAD_DIM))


def _swa_prompt_group(q, k, v, g):
    nb, t, _ = q.shape
    window, dil = SWA_GROUPS[g]
    assert window // dil == SWA_BAND and t % (SWA_BAND * dil) == 0
    m_len = t // dil
    view = lambda x: x.reshape(nb, m_len, dil * SWA_W)
    nblk = SWA_W // GROUP_W
    cur = pl.BlockSpec((1, SWA_BAND, GROUP_W), lambda b, n, r: (b, n, r * nblk + g))
    prev = pl.BlockSpec((1, SWA_BAND, GROUP_W), lambda b, n, r: (b, jnp.maximum(n - 1, 0), r * nblk + g))
    out = pl.BlockSpec((1, SWA_BAND, GROUP_W), lambda b, n, r: (b, n, r))
    o, lse = pl.pallas_call(
        _swa_prompt_kernel,
        grid=(nb, m_len // SWA_BAND, dil),
        in_specs=[cur, prev, cur, prev, cur],
        out_specs=[out, out],
        out_shape=[jax.ShapeDtypeStruct((nb, m_len, dil * GROUP_W), BF16),
                   jax.ShapeDtypeStruct((nb, m_len, dil * GROUP_W), F32)],
        compiler_params=_params("arbitrary", "arbitrary", "arbitrary"),
        name=f"swa_prompt_g{g}",
    )(view(q), view(k), view(k), view(v), view(v))
    return o.reshape(nb, t, GROUP_W), lse.reshape(nb, t, GROUP_W)


def _swa_sample_kernel(q_ref, kn_ref, vn_ref, k0_ref, v0_ref, k1_ref, v1_ref, k2_ref, v2_ref, o_ref):
    rows = lax.broadcasted_iota(jnp.int32, (SUBLANES, GROUP_W), 0)
    lanes = lax.broadcasted_iota(jnp.int32, (SUBLANES, GROUP_W), 1)
    own = (lanes >= rows * SWA_HEAD_DIM) & (lanes < (rows + 1) * SWA_HEAD_DIM)
    state = ((k0_ref, v0_ref), (k1_ref, v1_ref), (k2_ref, v2_ref))
    outs, lses = [], []
    for g in range(len(SWA_GROUPS)):
        sl = slice(g * GROUP_W, (g + 1) * GROUP_W)
        qg = jnp.where(own, q_ref[0][:, sl].astype(F32), 0.0)
        kn, vn = kn_ref[0][:, sl], vn_ref[0][:, sl]
        kb, vb = state[g][0][0].astype(BF16), state[g][1][0].astype(BF16)
        s = _nt_dot(qg.astype(BF16), kb)
        s_n = jnp.sum(qg * kn, axis=-1, keepdims=True)
        m = jnp.maximum(jnp.max(s, axis=-1, keepdims=True), s_n)
        p, p_n = jnp.exp(s - m), jnp.exp(s_n - m)
        l = jnp.sum(p, axis=-1, keepdims=True) + p_n
        o = (_dot(p.astype(BF16), vb) + p_n * vn) / l
        lse = jnp.broadcast_to(m + jnp.log(l), o.shape)
        outs.append(jnp.sum(jnp.where(own, o, 0.0), axis=0, keepdims=True))
        lses.append(jnp.sum(jnp.where(own, lse, 0.0), axis=0, keepdims=True))
    alphas = _group_softmax(jnp.concatenate(lses, axis=1))
    o_ref[0] = jnp.concatenate([outs[g] * alphas[g] for g in range(len(SWA_GROUPS))], axis=1).astype(BF16)


def _swa_sample(q, kn, vn, states):
    db = q.shape[0]
    tok = pl.BlockSpec((1, 1, SWA_W), lambda b: (b, 0, 0))
    views, specs = [], []
    for g, (window, dil) in enumerate(SWA_GROUPS):
        for buf in states[g]:
            assert buf.shape[1] == window and window // dil == SWA_BAND
            views.append(buf.reshape(db, SWA_BAND, dil * GROUP_W))
            specs.append(pl.BlockSpec((1, SWA_BAND, GROUP_W), lambda b: (b, 0, 0)))
    return pl.pallas_call(
        _swa_sample_kernel,
        grid=(db,),
        in_specs=[tok, tok, tok] + specs,
        out_specs=tok,
        out_shape=jax.ShapeDtypeStruct((db, 1, SWA_W), BF16),
        compiler_params=_params("arbitrary"),
        name="swa_sample",
    )(q, kn, vn, *views)


def kernel(x_prompt, x_sample, c_prompt, c_sample, page_table, cache_mla_latent, cache_mla_krope, state_swa_k0, state_swa_v0, state_swa_k1, state_swa_v1, state_swa_k2, state_swa_v2, g_norm_mix, g_norm_ff, w_ada, b_ada, w_ff1, w_ff2, w_mla_in, g_mla_qa, g_mla_kva, w_mla_uq, w_mla_uk, w_mla_uv, g_mla_qn, g_mla_qr, g_mla_kn, g_mla_kr, w_mla_o, g_kv_norm, w_ada_kv, b_ada_kv, w_kv, g_swa_k, w_swa_q, g_swa_q, w_swa_o):
    nb, t, d = x_prompt.shape
    db, s_len, _ = x_sample.shape
    assert s_len == 1 and w_mla_in.shape[0] == 1 and w_swa_q.shape[0] == 1 and g_norm_mix.shape[0] == 2
    assert db % SUBLANES == 0 and cache_mla_latent.shape[2] == 1
    n_pages = page_table.shape[1]
    psz = cache_mla_latent.shape[1]
    tm = min(512, t)
    tq = min(1024, t)
    pp = 8 if n_pages % 8 == 0 else 2
    dff = w_ff1.shape[-1]
    tf = min(1024, dff)

    n_mod = -(-(db + nb) // SUBLANES) * SUBLANES
    c_all = jnp.concatenate([c_sample, c_prompt, jnp.zeros((n_mod - db - nb, d), F32)], axis=0)
    prm = _Rows(nb, t, tm, lambda n: (n_mod, 1, n), lambda w: (1, 1, w), lambda b, k: (db + b, 0, k))
    smp = _Rows(1, db, db, lambda n: (1, n_mod, n), lambda w: (1, db, w), lambda b, k: (0, 0, k))

    bf = lambda w: w.astype(BF16)
    pad_lanes = lambda w, n: jnp.pad(w, [(0, 0)] * (w.ndim - 1) + [(0, n - w.shape[-1])])
    row = lambda v: v.reshape(1, -1)

    pos_p = jnp.arange(t, dtype=F32)
    pos_s = jnp.full((db,), PAST_LEN, F32)
    rope_mla = {prm: _rope_tables(pos_p, ROPE_DIM), smp: _rope_tables(pos_s, ROPE_DIM)}
    rope_swa = {prm: _rope_tables(pos_p, ROT_DIM), smp: _rope_tables(pos_s, ROT_DIM)}

    xs = {prm: x_prompt, smp: x_sample.reshape(1, db, d)}

    mod0 = _ada(c_all, w_ada[0], b_ada[0])
    w_in = bf(pad_lanes(w_mla_in[0], Q_LORA + KV_LORA + LANES))
    g_kr = pad_lanes(row(g_mla_kr[0]), LANES)
    g_qr = pad_lanes(row(g_mla_qr[0]), LANES)
    w_uq = w_mla_uq[0]
    w_uq_n = bf(w_uq[:, :, :NOPE_DIM].reshape(Q_LORA, MLA_W))
    w_uq_r = bf(pad_lanes(w_uq[:, :, NOPE_DIM:], LANES).reshape(Q_LORA, MLA_W))
    w_uk = bf(w_mla_uk[0].reshape(KV_LORA, MLA_W))
    w_uv = bf(w_mla_uv[0].reshape(KV_LORA, MLA_W))
    w_uk_t3 = bf(jnp.transpose(w_mla_uk[0], (1, 2, 0)))
    w_uv_3 = bf(jnp.transpose(w_mla_uv[0], (1, 0, 2)))
    w_o = bf(w_mla_o[0].reshape(MLA_W, d))
    w1, w2 = bf(w_ff1), bf(w_ff2)

    proj = {}
    for rows in (prm, smp):
        cq, lat, latb, kr, krb = _mla_in(rows, xs[rows], row(g_norm_mix[0]), mod0, w_in, row(g_mla_qa[0]),
                                         row(g_mla_kva[0]), g_kr, rope_mla[rows])
        q = _q_up(rows, cq, w_uq_n, w_uq_r, row(g_mla_qn[0]), g_qr, rope_mla[rows])
        kc, v = _kv_up(rows, latb, krb, w_uk, w_uv, row(g_mla_kn[0]))
        proj[rows] = (q, kc, v, lat, kr)

    q, kc, v, lat_p, kr_p = proj[prm]
    mix_p = _flash(q, kc, v, tq)

    q, kc, _, lat_s, kr_s = proj[smp]
    wq = jnp.transpose(_absorb(q, row(g_mla_kn[0]), w_uk_t3), (1, 0, 2))
    q_b = jnp.transpose(q[0], (1, 0, 2)).astype(F32)
    kc_b = jnp.transpose(kc[0], (1, 0, 2)).astype(F32)
    o_lat = _decode(page_table, cache_mla_latent.reshape(-1, psz, KV_LORA),
                    cache_mla_krope.reshape(-1, psz, ROPE_DIM), w_uk_t3.reshape(MLA_W, KV_LORA),
                    wq, q_b, kc_b, lat_s.reshape(db, 1, KV_LORA), pp)
    mix_s = _uv(jnp.transpose(o_lat, (1, 0, 2)), w_uv_3).reshape(1, db, MLA_W)

    mixes = {prm: mix_p, smp: mix_s}
    for rows in (prm, smp):
        x = _proj_res(rows, mixes[rows], w_o, xs[rows], mod0, 2)
        xs[rows] = _mlp(rows, x, row(g_norm_ff[0]), mod0, w1[0], w2[0], tf)

    mod_kv = _ada(c_all, w_ada_kv, b_ada_kv)
    mod1 = _ada(c_all, w_ada[1], b_ada[1])
    w_kv2 = bf(w_kv.reshape(d, 2 * SWA_W))
    w_q = bf(w_swa_q[0].reshape(d, SWA_W))
    w_so = bf(w_swa_o[0].reshape(SWA_W, d))
    kv, qs = {}, {}
    for rows in (prm, smp):
        kv[rows] = _shared_kv(rows, xs[rows], row(g_kv_norm), mod_kv, w_kv2, row(g_swa_k), rope_swa[rows])
        qs[rows] = _swa_q(rows, xs[rows], row(g_norm_mix[1]), mod1, w_q, row(g_swa_q[0]), rope_swa[rows])

    k_p, kb_p, v_p, vb_p = kv[prm]
    groups = [_swa_prompt_group(qs[prm], kb_p, vb_p, g) for g in range(len(SWA_GROUPS))]
    o_p = jnp.concatenate([o for o, _ in groups], axis=-1)
    lse_p = jnp.concatenate([l for _, l in groups], axis=-1)
    xs[prm] = _swa_proj_res(prm, o_p, lse_p, w_so, xs[prm], mod1, 2)

    k_s, _, v_s, _ = kv[smp]
    states = ((state_swa_k0, state_swa_v0), (state_swa_k1, state_swa_v1), (state_swa_k2, state_swa_v2))
    o_s = _swa_sample(qs[smp].reshape(db, 1, SWA_W), k_s.reshape(db, 1, SWA_W), v_s.reshape(db, 1, SWA_W), states)
    xs[smp] = _proj_res(smp, o_s.reshape(1, db, SWA_W), w_so, xs[smp], mod1, 2)

    for rows in (prm, smp):
        xs[rows] = _mlp(rows, xs[rows], row(g_norm_ff[1]), mod1, w1[1], w2[1], tf)

    k_p4 = k_p.reshape(nb, t, N_SWA_HEADS, SWA_HEAD_DIM)
    v_p4 = v_p.reshape(nb, t, N_SWA_HEADS, SWA_HEAD_DIM)
    k_s4 = k_s.reshape(db, 1, N_SWA_HEADS, SWA_HEAD_DIM)
    v_s4 = v_s.reshape(db, 1, N_SWA_HEADS, SWA_HEAD_DIM)
    heads = lambda x, g: x[:, :, g * HEADS_PER_GROUP:(g + 1) * HEADS_PER_GROUP]
    tail = lambda x, g: heads(x, g)[:, t - min(SWA_GROUPS[g][0], t):]
    outs = [xs[prm], xs[smp].reshape(db, 1, d),
            lat_p.reshape(nb, t, 1, KV_LORA), kr_p.reshape(nb, t, 1, ROPE_DIM),
            lat_s.reshape(db, 1, 1, KV_LORA), kr_s.reshape(db, 1, 1, ROPE_DIM)]
    for g in range(len(SWA_GROUPS)):
        outs += [tail(k_p4, g), tail(v_p4, g)]
    for g in range(len(SWA_GROUPS)):
        outs += [heads(k_s4, g), heads(v_s4, g)]
    return tuple(outs)
```

```python
import functools

import jax
import jax.numpy as jnp
from jax import lax
from jax.experimental import pallas as pl
from jax.experimental.pallas import tpu as pltpu

F32 = jnp.float32
BF16 = jnp.bfloat16

MLA_HEADS = 16
Q_LORA = 512
KV_LORA = 512
NOPE_DIM = 128
ROPE_DIM = 64
V_DIM = 128
SWA_GROUPS = ((128, 1), (512, 4), (2048, 16))
HEADS_PER_GROUP = 4
N_SWA_HEADS = HEADS_PER_GROUP * len(SWA_GROUPS)
SWA_HEAD_DIM = 128
ROT_DIM = SWA_HEAD_DIM // 4
ROPE_THETA = 500000.0
PAST_LEN = 16384
EPS = 1e-6
MLA_SCALE = (NOPE_DIM + ROPE_DIM) ** -0.5
SWA_SCALE = SWA_HEAD_DIM ** -0.5

LANES = 128
SUBLANES = 8
VMEM_LIMIT_BYTES = 56 * 2**20
SWA_BAND = 128
GROUP_W = HEADS_PER_GROUP * SWA_HEAD_DIM
SWA_W = N_SWA_HEADS * SWA_HEAD_DIM
MLA_W = MLA_HEADS * LANES
MLA_QK = 2 * LANES
FLASH_HEADS = 2
DECODE_PAGES = 16
DECODE_CHUNK = 4
LAT_SPLIT = KV_LORA // LANES
NEG_INF = float("-inf")


def _params(*sem):
    return pltpu.CompilerParams(dimension_semantics=sem, vmem_limit_bytes=VMEM_LIMIT_BYTES)


def _nt_dot(a, b):
    return lax.dot_general(a, b, (((1,), (1,)), ((), ())), preferred_element_type=F32)


def _dot(a, b):
    return jnp.dot(a, b, preferred_element_type=F32)


def _rms(x, g, n):
    ms = jnp.sum(x * x, axis=-1, keepdims=True) * (1.0 / n)
    return x * lax.rsqrt(ms + EPS) * g


def _rope(x, cos, sin_lo, sin_hi, half):
    return x * cos + pltpu.roll(x, LANES - half, 1) * sin_lo + pltpu.roll(x, half, 1) * sin_hi


def _norm_mod(x, g, shift, scale):
    return _rms(x, g, x.shape[-1]) * (1.0 + scale) + shift


def _rope_tables(pos, rot_dim):
    half = rot_dim // 2
    inv_freq = ROPE_THETA ** (-(jnp.arange(half, dtype=F32) * 2.0 / rot_dim))
    ang = pos.astype(F32)[:, None] * inv_freq[None, :]
    cos, sin = jnp.cos(ang), jnp.sin(ang)
    t = pos.shape[0]
    zeros = lambda n: jnp.zeros((t, n), F32)
    cos_t = jnp.concatenate([cos, cos, jnp.ones((t, LANES - rot_dim), F32)], axis=1)
    sin_lo = jnp.concatenate([-sin, zeros(LANES - half)], axis=1)
    sin_hi = jnp.concatenate([zeros(half), sin, zeros(LANES - rot_dim)], axis=1)
    return cos_t, sin_lo, sin_hi


class _Rows:
    def __init__(self, nb, t, tm, mod_view, mod_block, mod_index):
        self.nb, self.t, self.tm = nb, t, tm
        self._mod_view, self._mod_block, self._mod_index = mod_view, mod_block, mod_index

    def mod_arr(self, mod):
        return mod.reshape(self._mod_view(mod.shape[1]))

    def mod_spec(self, k, d):
        idx = self._mod_index
        return pl.BlockSpec(self._mod_block(d), lambda b, *_: idx(b, k))

    def row_spec(self, width):
        return pl.BlockSpec((1, self.tm, width), lambda b, i, *_: (b, i, 0))

    def tab_spec(self):
        return pl.BlockSpec((self.tm, LANES), lambda b, i, *_: (i, 0))


def _full(shape):
    return pl.BlockSpec(shape, lambda *_: (0,) * len(shape))


def _ada_kernel(c_ref, w_ref, b_ref, o_ref):
    c = c_ref[...]
    s = c / (1.0 + jnp.exp(-c))
    o_ref[...] = _dot(s.astype(BF16), w_ref[...].astype(BF16)) + b_ref[...]


def _ada(c, w, b, layer, tn=1024):
    m, d = c.shape
    nl, _, n = w.shape
    return pl.pallas_call(
        _ada_kernel,
        grid=(n // tn,),
        in_specs=[pl.BlockSpec((m, d), lambda j: (0, 0)),
                  pl.BlockSpec((None, d, tn), lambda j: (layer, 0, j)),
                  pl.BlockSpec((None, 1, tn), lambda j: (layer, 0, j))],
        out_specs=pl.BlockSpec((m, tn), lambda j: (0, j)),
        out_shape=jax.ShapeDtypeStruct((m, n), F32),
        compiler_params=_params("arbitrary"),
        name="ada_mod",
    )(c, w, b.reshape(nl, 1, n))


def _mla_in_kernel(x_ref, g_ref, sh_ref, sc_ref, w_ref, gqa_ref, gkva_ref, gkr_ref,
                   cos_ref, slo_ref, shi_ref, cq_ref, lat_ref, latb_ref, kr_ref, krb_ref):
    xn = _norm_mod(x_ref[0], g_ref[...], sh_ref[0], sc_ref[0])
    a = _dot(xn.astype(BF16), w_ref[...])
    cq_ref[0] = _rms(a[:, :Q_LORA], gqa_ref[...], Q_LORA).astype(BF16)
    lat = _rms(a[:, Q_LORA:Q_LORA + KV_LORA], gkva_ref[...], KV_LORA)
    lat_ref[0] = lat
    latb_ref[0] = lat.astype(BF16)
    kr = _rms(a[:, Q_LORA + KV_LORA:], gkr_ref[...], ROPE_DIM)
    kr = _rope(kr, cos_ref[...], slo_ref[...], shi_ref[...], ROPE_DIM // 2)
    kr_ref[0] = kr[:, :ROPE_DIM]
    krb_ref[0] = kr.astype(BF16)


def _mla_in(rows, x, g, mod, w, gqa, gkva, gkr, rope):
    d = x.shape[-1]
    n = w.shape[1]
    marr = rows.mod_arr(mod)
    shape = lambda width, dt: jax.ShapeDtypeStruct((rows.nb, rows.t, width), dt)
    return pl.pallas_call(
        _mla_in_kernel,
        grid=(rows.nb, rows.t // rows.tm),
        in_specs=[rows.row_spec(d), _full((1, d)), rows.mod_spec(0, d), rows.mod_spec(1, d),
                  _full((d, n)), _full((1, Q_LORA)), _full((1, KV_LORA)), _full((1, LANES)),
                  rows.tab_spec(), rows.tab_spec(), rows.tab_spec()],
        out_specs=[rows.row_spec(Q_LORA), rows.row_spec(KV_LORA), rows.row_spec(KV_LORA),
                   rows.row_spec(ROPE_DIM), rows.row_spec(LANES)],
        out_shape=[shape(Q_LORA, BF16), shape(KV_LORA, F32), shape(KV_LORA, BF16),
                   shape(ROPE_DIM, F32), shape(LANES, BF16)],
        compiler_params=_params("arbitrary", "arbitrary"),
        name="mla_in",
    )(x, g, marr, marr, w, gqa, gkva, gkr, *rope)


def _q_up_kernel(cq_ref, wn_ref, wr_ref, gqn_ref, gqr_ref, cos_ref, slo_ref, shi_ref, q_ref):
    cq = cq_ref[0]
    an = _dot(cq, wn_ref[...])
    ar = _dot(cq, wr_ref[...])
    cos, slo, shi = cos_ref[...], slo_ref[...], shi_ref[...]
    for h in range(MLA_HEADS):
        sl = slice(h * LANES, (h + 1) * LANES)
        qn = _rms(an[:, sl], gqn_ref[...], NOPE_DIM) * MLA_SCALE
        qr = _rope(_rms(ar[:, sl], gqr_ref[...], ROPE_DIM), cos, slo, shi, ROPE_DIM // 2) * MLA_SCALE
        q_ref[0, h, :, 0:LANES] = qn.astype(BF16)
        q_ref[0, h, :, LANES:MLA_QK] = qr.astype(BF16)


def _q_up(rows, cq, wn, wr, gqn, gqr, rope):
    tm = rows.tm
    return pl.pallas_call(
        _q_up_kernel,
        grid=(rows.nb, rows.t // tm),
        in_specs=[rows.row_spec(Q_LORA), _full((Q_LORA, MLA_W)), _full((Q_LORA, MLA_W)),
                  _full((1, LANES)), _full((1, LANES)),
                  rows.tab_spec(), rows.tab_spec(), rows.tab_spec()],
        out_specs=pl.BlockSpec((1, MLA_HEADS, tm, MLA_QK), lambda b, i: (b, 0, i, 0)),
        out_shape=jax.ShapeDtypeStruct((rows.nb, MLA_HEADS, rows.t, MLA_QK), BF16),
        compiler_params=_params("arbitrary", "arbitrary"),
        name="mla_q_up",
    )(cq, wn, wr, gqn, gqr, *rope)


def _kv_up_kernel(latb_ref, krb_ref, wk_ref, wv_ref, gkn_ref, k_ref, v_ref):
    lat = latb_ref[0]
    ak = _dot(lat, wk_ref[...])
    av = _dot(lat, wv_ref[...])
    krb = krb_ref[0]
    for h in range(MLA_HEADS):
        sl = slice(h * LANES, (h + 1) * LANES)
        k_ref[0, h, :, 0:LANES] = _rms(ak[:, sl], gkn_ref[...], NOPE_DIM).astype(BF16)
        k_ref[0, h, :, LANES:MLA_QK] = krb
        v_ref[0, h] = av[:, sl].astype(BF16)


def _kv_up(rows, latb, krb, wk, wv, gkn):
    tm = rows.tm
    hspec = lambda w: pl.BlockSpec((1, MLA_HEADS, tm, w), lambda b, i: (b, 0, i, 0))
    return pl.pallas_call(
        _kv_up_kernel,
        grid=(rows.nb, rows.t // tm),
        in_specs=[rows.row_spec(KV_LORA), rows.row_spec(LANES), _full((KV_LORA, MLA_W)),
                  _full((KV_LORA, MLA_W)), _full((1, LANES))],
        out_specs=[hspec(MLA_QK), hspec(V_DIM)],
        out_shape=[jax.ShapeDtypeStruct((rows.nb, MLA_HEADS, rows.t, MLA_QK), BF16),
                   jax.ShapeDtypeStruct((rows.nb, MLA_HEADS, rows.t, V_DIM), BF16)],
        compiler_params=_params("arbitrary", "arbitrary"),
        name="mla_kv_up",
    )(latb, krb, wk, wv, gkn)


def _flash_kernel(qi_ref, ki_ref, q_ref, k_ref, v_ref, o_ref, m_ref, l_ref, acc_ref):
    s_idx = pl.program_id(2)
    qi, ki = qi_ref[s_idx], ki_ref[s_idx]

    @pl.when(ki == 0)
    def _():
        m_ref[...] = jnp.full(m_ref.shape, NEG_INF, F32)
        l_ref[...] = jnp.zeros(l_ref.shape, F32)
        acc_ref[...] = jnp.zeros(acc_ref.shape, F32)

    def update(diagonal):
        for h in range(FLASH_HEADS):
            s = _nt_dot(q_ref[0, h], k_ref[0, h])
            if diagonal:
                rows = lax.broadcasted_iota(jnp.int32, s.shape, 0)
                cols = lax.broadcasted_iota(jnp.int32, s.shape, 1)
                s = jnp.where(cols <= rows, s, NEG_INF)
            m_prev = m_ref[h]
            m_new = jnp.maximum(m_prev, jnp.max(s, axis=-1, keepdims=True))
            corr = jnp.exp(m_prev - m_new)
            p = jnp.exp(s - m_new)
            l_ref[h] = corr * l_ref[h] + jnp.sum(p, axis=-1, keepdims=True)
            acc_ref[h] = corr * acc_ref[h] + _dot(p.astype(BF16), v_ref[0, h])
            m_ref[h] = m_new

    @pl.when(ki < qi)
    def _():
        update(False)

    @pl.when(ki == qi)
    def _():
        update(True)
        for h in range(FLASH_HEADS):
            o_ref[0, :, h * V_DIM:(h + 1) * V_DIM] = (acc_ref[h] / l_ref[h]).astype(o_ref.dtype)


def _flash(q, k, v, tq):
    nb, nh, t, _ = q.shape
    nq = t // tq
    fh = FLASH_HEADS
    pairs = [(i, j) for i in range(nq) for j in range(i + 1)]
    qi = jnp.asarray([p[0] for p in pairs], jnp.int32)
    ki = jnp.asarray([p[1] for p in pairs], jnp.int32)
    grid_spec = pltpu.PrefetchScalarGridSpec(
        num_scalar_prefetch=2,
        grid=(nb, nh // fh, len(pairs)),
        in_specs=[pl.BlockSpec((1, fh, tq, MLA_QK), lambda b, h, s, qi, ki: (b, h, qi[s], 0)),
                  pl.BlockSpec((1, fh, tq, MLA_QK), lambda b, h, s, qi, ki: (b, h, ki[s], 0)),
                  pl.BlockSpec((1, fh, tq, V_DIM), lambda b, h, s, qi, ki: (b, h, ki[s], 0))],
        out_specs=pl.BlockSpec((1, tq, fh * V_DIM), lambda b, h, s, qi, ki: (b, qi[s], h)),
        scratch_shapes=[pltpu.VMEM((fh, tq, 1), F32), pltpu.VMEM((fh, tq, 1), F32),
                        pltpu.VMEM((fh, tq, V_DIM), F32)],
    )
    return pl.pallas_call(
        _flash_kernel,
        grid_spec=grid_spec,
        out_shape=jax.ShapeDtypeStruct((nb, t, nh * V_DIM), BF16),
        compiler_params=_params("arbitrary", "arbitrary", "arbitrary"),
        name="mla_flash",
    )(qi, ki, q, k, v)


def _proj_res_kernel(a_ref, w_ref, x_ref, gate_ref, o_ref):
    o_ref[0] = x_ref[0] + gate_ref[0] * _dot(a_ref[0], w_ref[...])


def _proj_res(rows, a, w, x, mod, gate_k):
    kdim, d = w.shape
    return pl.pallas_call(
        _proj_res_kernel,
        grid=(rows.nb, rows.t // rows.tm),
        in_specs=[rows.row_spec(kdim), _full((kdim, d)), rows.row_spec(d), rows.mod_spec(gate_k, d)],
        out_specs=rows.row_spec(d),
        out_shape=jax.ShapeDtypeStruct((rows.nb, rows.t, d), F32),
        compiler_params=_params("arbitrary", "arbitrary"),
        name="proj_residual",
    )(a, w, x, rows.mod_arr(mod))


def _dilated_rows(r, tm, dil):
    return pl.ds(r, tm // dil, stride=dil) if dil > 1 else pl.ds(0, tm)


def _dilated_spec(tm, dil):
    return pl.BlockSpec((1, dil, tm // dil, GROUP_W), lambda b, i, *_: (b, 0, i, 0))


def _swa_proj_res_kernel(o0_ref, o1_ref, o2_ref, l0_ref, l1_ref, l2_ref, w_ref, x_ref, gate_ref, out_ref,
                         a_scr, lse_scr):
    tm = a_scr.shape[1]
    for g, (o_ref, l_ref) in enumerate(((o0_ref, l0_ref), (o1_ref, l1_ref), (o2_ref, l2_ref))):
        dil = SWA_GROUPS[g][1]
        for r in range(dil):
            for j in range(HEADS_PER_GROUP):
                sl = slice(j * SWA_HEAD_DIM, (j + 1) * SWA_HEAD_DIM)
                a_scr[g * HEADS_PER_GROUP + j, _dilated_rows(r, tm, dil), :] = o_ref[0, r][:, sl].astype(F32)
                lse_scr[g * HEADS_PER_GROUP + j, _dilated_rows(r, tm, dil), :] = l_ref[0, r][:, sl]
    mixed = [None] * N_SWA_HEADS
    for j in range(HEADS_PER_GROUP):
        heads = [g * HEADS_PER_GROUP + j for g in range(len(SWA_GROUPS))]
        lses = [lse_scr[h] for h in heads]
        mx = jnp.maximum(jnp.maximum(lses[0], lses[1]), lses[2])
        es = [jnp.exp(x - mx) for x in lses]
        inv = 1.0 / (es[0] + es[1] + es[2])
        for h, e in zip(heads, es):
            mixed[h] = (a_scr[h] * (e * inv)).astype(BF16)
    out_ref[0] = x_ref[0] + gate_ref[0] * _dot(jnp.concatenate(mixed, axis=1), w_ref[...])


def _swa_proj_res(rows, os_, lses, w, x, mod, gate_k):
    kdim, d = w.shape
    tm = rows.tm
    dspecs = [_dilated_spec(tm, dil) for _, dil in SWA_GROUPS]
    return pl.pallas_call(
        _swa_proj_res_kernel,
        grid=(rows.nb, rows.t // tm),
        in_specs=dspecs + dspecs + [_full((kdim, d)), rows.row_spec(d), rows.mod_spec(gate_k, d)],
        out_specs=rows.row_spec(d),
        out_shape=jax.ShapeDtypeStruct((rows.nb, rows.t, d), F32),
        scratch_shapes=[pltpu.VMEM((N_SWA_HEADS, tm, SWA_HEAD_DIM), F32)] * 2,
        compiler_params=_params("arbitrary", "arbitrary"),
        name="swa_proj_residual",
    )(*os_, *lses, w, x, rows.mod_arr(mod))


def _mlp_kernel(x_ref, g_ref, sh_ref, sc_ref, gate_ref, w1_ref, w2_ref, o_ref, xn_ref, acc_ref):
    f = pl.program_id(2)

    @pl.when(f == 0)
    def _():
        xn_ref[...] = _norm_mod(x_ref[0], g_ref[...], sh_ref[0], sc_ref[0]).astype(BF16)
        acc_ref[...] = jnp.zeros(acc_ref.shape, F32)

    h = jnp.maximum(_dot(xn_ref[...], w1_ref[...]), 0.0)
    acc_ref[...] += _dot((h * h).astype(BF16), w2_ref[...])

    @pl.when(f == pl.num_programs(2) - 1)
    def _():
        o_ref[0] = x_ref[0] + gate_ref[0] * acc_ref[...]


def _mlp(rows, x, g, mod, w1, w2, layer, tf):
    _, d, dff = w1.shape
    tm = rows.tm
    marr = rows.mod_arr(mod)
    return pl.pallas_call(
        _mlp_kernel,
        grid=(rows.nb, rows.t // tm, dff // tf),
        in_specs=[rows.row_spec(d), _full((1, d)), rows.mod_spec(3, d), rows.mod_spec(4, d),
                  rows.mod_spec(5, d),
                  pl.BlockSpec((None, d, tf), lambda b, i, f: (layer, 0, f)),
                  pl.BlockSpec((None, tf, d), lambda b, i, f: (layer, f, 0))],
        out_specs=rows.row_spec(d),
        out_shape=jax.ShapeDtypeStruct((rows.nb, rows.t, d), F32),
        scratch_shapes=[pltpu.VMEM((tm, d), BF16), pltpu.VMEM((tm, d), F32)],
        compiler_params=_params("arbitrary", "arbitrary", "arbitrary"),
        name="mlp",
    )(x, g, marr, marr, marr, w1, w2)


def _decode_kernel(pp, chunk, pt_ref, *refs):
    lat_refs, kr_refs = refs[:pp], refs[pp:2 * pp]
    (wukt_ref, wq_ref, q_ref, kc_ref, latn_ref, o_ref, m_ref, l_ref, acc_ref) = refs[2 * pp:]
    j = pl.program_id(1)

    @pl.when(j == 0)
    def _():
        m_ref[...] = jnp.full(m_ref.shape, NEG_INF, F32)
        l_ref[...] = jnp.zeros(l_ref.shape, F32)
        acc_ref[...] = jnp.zeros(acc_ref.shape, F32)

    def page(ref):
        n = ref.shape[0] // LAT_SPLIT
        return jnp.concatenate([ref[pl.ds(c, n, stride=LAT_SPLIT), :] for c in range(LAT_SPLIT)], axis=1)

    wq = wq_ref[0]
    qr = q_ref[0][:, LANES:LANES + ROPE_DIM].astype(BF16)
    lats, scores = [], []
    for i in range(0, pp, chunk):
        lat = jnp.concatenate([page(lat_refs[i + c]) for c in range(chunk)], axis=0).astype(BF16)
        kr_t = jnp.concatenate([kr_refs[i + c][0] for c in range(chunk)], axis=1).astype(BF16)
        raw_t = _nt_dot(wukt_ref[...], lat)
        raw3 = raw_t.reshape(MLA_HEADS, NOPE_DIM, raw_t.shape[-1])
        ssq = jnp.sum(raw3 * raw3, axis=1)
        scores.append(_nt_dot(wq, lat) * lax.rsqrt(ssq * (1.0 / NOPE_DIM) + EPS) + _dot(qr, kr_t))
        lats.append(lat)
    s = jnp.concatenate(scores, axis=1)
    m_prev = m_ref[...]
    m_new = jnp.maximum(m_prev, jnp.max(s, axis=-1, keepdims=True))
    corr = jnp.exp(m_prev - m_new)
    p = jnp.exp(s - m_new)
    l_ref[...] = corr * l_ref[...] + jnp.sum(p, axis=-1, keepdims=True)
    acc_ref[...] = corr * acc_ref[...] + _dot(p.astype(BF16), jnp.concatenate(lats, axis=0))
    m_ref[...] = m_new

    @pl.when(j == pl.num_programs(1) - 1)
    def _():
        s_n = jnp.sum(q_ref[0] * kc_ref[0], axis=-1, keepdims=True)
        m_prev = m_ref[...]
        m_new = jnp.maximum(m_prev, s_n)
        corr = jnp.exp(m_prev - m_new)
        p_n = jnp.exp(s_n - m_new)
        l_fin = corr * l_ref[...] + p_n
        acc = corr * acc_ref[...] + p_n * latn_ref[0]
        o_ref[0] = acc / l_fin


def _decode(page_table, cache_lat, cache_kr, wukt, wq, q, kc, latn, pp, chunk):
    db, n_pages = page_table.shape
    psz = cache_kr.shape[2]
    pt = page_table.reshape(-1)
    phys = lambda b, j, pt, i: pt[b * n_pages + j * pp + i]
    lat_spec = lambda i: pl.BlockSpec((psz * LAT_SPLIT, LANES), lambda b, j, pt: (phys(b, j, pt, i), 0))
    kr_spec = lambda i: pl.BlockSpec((1, ROPE_DIM, psz), lambda b, j, pt: (phys(b, j, pt, i), 0, 0))
    per_b = lambda shape: pl.BlockSpec((1,) + shape, lambda b, j, pt: (b, 0, 0))
    grid_spec = pltpu.PrefetchScalarGridSpec(
        num_scalar_prefetch=1,
        grid=(db, n_pages // pp),
        in_specs=([lat_spec(i) for i in range(pp)] + [kr_spec(i) for i in range(pp)]
                  + [pl.BlockSpec(wukt.shape, lambda b, j, pt: (0, 0)),
                     per_b((MLA_HEADS, KV_LORA)), per_b((MLA_HEADS, MLA_QK)), per_b((MLA_HEADS, MLA_QK)),
                     per_b((1, KV_LORA))]),
        out_specs=per_b((MLA_HEADS, KV_LORA)),
        scratch_shapes=[pltpu.VMEM((MLA_HEADS, 1), F32), pltpu.VMEM((MLA_HEADS, 1), F32),
                        pltpu.VMEM((MLA_HEADS, KV_LORA), F32)],
    )
    return pl.pallas_call(
        functools.partial(_decode_kernel, pp, chunk),
        grid_spec=grid_spec,
        out_shape=jax.ShapeDtypeStruct((db, MLA_HEADS, KV_LORA), F32),
        compiler_params=_params("arbitrary", "arbitrary"),
        name="mla_decode",
    )(pt, *([cache_lat] * pp), *([cache_kr] * pp), wukt, wq, q, kc, latn)


def _absorb_kernel(q_ref, gkn_ref, wukt_ref, o_ref):
    qg = (q_ref[0, 0][:, :LANES].astype(F32) * gkn_ref[...]).astype(BF16)
    o_ref[0] = _dot(qg, wukt_ref[0]).astype(BF16)


def _absorb(q, gkn, wukt3):
    _, nh, db, _ = q.shape
    return pl.pallas_call(
        _absorb_kernel,
        grid=(nh,),
        in_specs=[pl.BlockSpec((1, 1, db, MLA_QK), lambda h: (0, h, 0, 0)), _full((1, LANES)),
                  pl.BlockSpec((1, NOPE_DIM, KV_LORA), lambda h: (h, 0, 0))],
        out_specs=pl.BlockSpec((1, db, KV_LORA), lambda h: (h, 0, 0)),
        out_shape=jax.ShapeDtypeStruct((nh, db, KV_LORA), BF16),
        compiler_params=_params("arbitrary"),
        name="mla_absorb_q",
    )(q, gkn, wukt3)


def _uv_kernel(o_ref, w_ref, out_ref):
    out_ref[...] = _dot(o_ref[0].astype(BF16), w_ref[0]).astype(BF16)


def _uv(o_lat, wuv3):
    nh, db, _ = o_lat.shape
    return pl.pallas_call(
        _uv_kernel,
        grid=(nh,),
        in_specs=[pl.BlockSpec((1, db, KV_LORA), lambda h: (h, 0, 0)),
                  pl.BlockSpec((1, KV_LORA, V_DIM), lambda h: (h, 0, 0))],
        out_specs=pl.BlockSpec((db, V_DIM), lambda h: (0, h)),
        out_shape=jax.ShapeDtypeStruct((db, nh * V_DIM), BF16),
        compiler_params=_params("arbitrary"),
        name="mla_uv",
    )(o_lat, wuv3)


def _head_norm_rope(a, g, cos, slo, shi, scale):
    outs = []
    for h in range(a.shape[-1] // LANES):
        y = _rope(_rms(a[:, h * LANES:(h + 1) * LANES], g, SWA_HEAD_DIM), cos, slo, shi, ROT_DIM // 2)
        outs.append(y * scale if scale != 1.0 else y)
    return outs


def _write_dilated(heads, scr_ref, out_refs):
    tm = scr_ref.shape[1]
    for h, y in enumerate(heads):
        scr_ref[h] = y
    for g, (_, dil) in enumerate(SWA_GROUPS):
        for r in range(dil):
            for j in range(HEADS_PER_GROUP):
                rows = scr_ref[g * HEADS_PER_GROUP + j, _dilated_rows(r, tm, dil), :]
                out_refs[g][0, r, :, j * SWA_HEAD_DIM:(j + 1) * SWA_HEAD_DIM] = rows.astype(BF16)


def _dilated_outs(rows):
    specs = [_dilated_spec(rows.tm, dil) for _, dil in SWA_GROUPS]
    shapes = [jax.ShapeDtypeStruct((rows.nb, dil, rows.t // dil, GROUP_W), BF16) for _, dil in SWA_GROUPS]
    return specs, shapes


def _head_scratch(rows, dilated):
    return [pltpu.VMEM((N_SWA_HEADS, rows.tm, SWA_HEAD_DIM), F32)] if dilated else []


def _swa_q_kernel(dilated, x_ref, g_ref, sh_ref, sc_ref, w_ref, gq_ref, cos_ref, slo_ref, shi_ref,
                  q_ref, *rest):
    xn = _norm_mod(x_ref[0], g_ref[...], sh_ref[0], sc_ref[0])
    a = _dot(xn.astype(BF16), w_ref[...])
    heads = _head_norm_rope(a, gq_ref[...], cos_ref[...], slo_ref[...], shi_ref[...], SWA_SCALE)
    q_ref[0] = jnp.concatenate(heads, axis=1)
    if dilated:
        _write_dilated(heads, rest[-1], rest[:-1])


def _swa_q(rows, x, g, mod, w, gq, rope, dilated):
    d, n = w.shape
    marr = rows.mod_arr(mod)
    dspecs, dshapes = _dilated_outs(rows) if dilated else ([], [])
    return pl.pallas_call(
        functools.partial(_swa_q_kernel, dilated),
        grid=(rows.nb, rows.t // rows.tm),
        in_specs=[rows.row_spec(d), _full((1, d)), rows.mod_spec(0, d), rows.mod_spec(1, d),
                  _full((d, n)), _full((1, LANES)), rows.tab_spec(), rows.tab_spec(), rows.tab_spec()],
        out_specs=[rows.row_spec(n)] + dspecs,
        out_shape=[jax.ShapeDtypeStruct((rows.nb, rows.t, n), F32)] + dshapes,
        scratch_shapes=_head_scratch(rows, dilated),
        compiler_params=_params("arbitrary", "arbitrary"),
        name="swa_q",
    )(x, g, marr, marr, w, gq, *rope)


def _shared_kv_kernel(dilated, x_ref, g_ref, sh_ref, sc_ref, w_ref, gk_ref, cos_ref, slo_ref, shi_ref,
                      k_ref, v_ref, *rest):
    xn_ref = rest[2 * len(SWA_GROUPS) if dilated else 0]
    c = pl.program_id(2)
    ngrp = len(SWA_GROUPS)

    @pl.when(c == 0)
    def _():
        xn_ref[...] = _norm_mod(x_ref[0], g_ref[...], sh_ref[0], sc_ref[0]).astype(BF16)
        a = _dot(xn_ref[...], w_ref[...])
        heads = _head_norm_rope(a, gk_ref[...], cos_ref[...], slo_ref[...], shi_ref[...], 1.0)
        k_ref[0] = jnp.concatenate(heads, axis=1)
        if dilated:
            _write_dilated(heads, rest[-1], rest[:ngrp])

    @pl.when(c == 1)
    def _():
        v = _dot(xn_ref[...], w_ref[...])
        v_ref[0] = v
        if dilated:
            heads = [v[:, h * SWA_HEAD_DIM:(h + 1) * SWA_HEAD_DIM] for h in range(N_SWA_HEADS)]
            _write_dilated(heads, rest[-1], rest[ngrp:2 * ngrp])


def _shared_kv(rows, x, g, mod, w, gk, rope, dilated):
    d = x.shape[-1]
    marr = rows.mod_arr(mod)
    dspecs, dshapes = _dilated_outs(rows) if dilated else ([], [])
    return pl.pallas_call(
        functools.partial(_shared_kv_kernel, dilated),
        grid=(rows.nb, rows.t // rows.tm, 2),
        in_specs=[rows.row_spec(d), _full((1, d)), rows.mod_spec(0, d), rows.mod_spec(1, d),
                  pl.BlockSpec((d, SWA_W), lambda b, i, c: (0, c)), _full((1, LANES)),
                  rows.tab_spec(), rows.tab_spec(), rows.tab_spec()],
        out_specs=[rows.row_spec(SWA_W)] * 2 + dspecs + dspecs,
        out_shape=[jax.ShapeDtypeStruct((rows.nb, rows.t, SWA_W), F32)] * 2 + dshapes + dshapes,
        scratch_shapes=[pltpu.VMEM((rows.tm, d), BF16)] + _head_scratch(rows, dilated),
        compiler_params=_params("arbitrary", "arbitrary", "arbitrary"),
        name="shared_kv",
    )(x, g, marr, marr, w, gk, *rope)


def _swa_prompt_kernel(q_ref, kp_ref, kc_ref, vp_ref, vc_ref, o_ref, lse_ref):
    n = pl.program_id(2)
    band = SWA_BAND
    qi = lax.broadcasted_iota(jnp.int32, (band, 2 * band), 0)
    ki = lax.broadcasted_iota(jnp.int32, (band, 2 * band), 1)
    lag = qi + band - ki
    first_key = jnp.where(n > 0, 0, band)
    ok = (lag >= 0) & (lag <= band) & (ki >= first_key)
    for h in range(HEADS_PER_GROUP):
        sl = slice(h * SWA_HEAD_DIM, (h + 1) * SWA_HEAD_DIM)
        k = jnp.concatenate([kp_ref[0, 0][:, sl], kc_ref[0, 0][:, sl]], axis=0)
        v = jnp.concatenate([vp_ref[0, 0][:, sl], vc_ref[0, 0][:, sl]], axis=0)
        s = jnp.where(ok, _nt_dot(q_ref[0, 0][:, sl], k), NEG_INF)
        m = jnp.max(s, axis=-1, keepdims=True)
        p = jnp.exp(s - m)
        l = jnp.sum(p, axis=-1, keepdims=True)
        o_ref[0, 0, :, sl] = (_dot(p.astype(BF16), v) / l).astype(BF16)
        lse_ref[0, 0, :, sl] = jnp.broadcast_to(m + jnp.log(l), (band, SWA_HEAD_DIM))


def _swa_prompt_group(q, k, v, g):
    nb, dil, m_len, _ = q.shape
    assert SWA_GROUPS[g] == (SWA_BAND * dil, dil) and m_len % SWA_BAND == 0
    blk = (1, 1, SWA_BAND, GROUP_W)
    cur = pl.BlockSpec(blk, lambda b, r, n: (b, r, n, 0))
    prev = pl.BlockSpec(blk, lambda b, r, n: (b, r, jnp.maximum(n - 1, 0), 0))
    return pl.pallas_call(
        _swa_prompt_kernel,
        grid=(nb, dil, m_len // SWA_BAND),
        in_specs=[cur, prev, cur, prev, cur],
        out_specs=[cur, cur],
        out_shape=[jax.ShapeDtypeStruct(q.shape, BF16), jax.ShapeDtypeStruct(q.shape, F32)],
        compiler_params=_params("arbitrary", "arbitrary", "arbitrary"),
        name=f"swa_prompt_g{g}",
    )(q, k, k, v, v)


def _swa_sample_kernel(q_ref, kn_ref, vn_ref, k0_ref, v0_ref, k1_ref, v1_ref, k2_ref, v2_ref, o_ref):
    state = ((k0_ref, v0_ref), (k1_ref, v1_ref), (k2_ref, v2_ref))
    outs, lses = [], []
    for g in range(len(SWA_GROUPS)):
        kb, vb = state[g][0][...], state[g][1][...]
        kb = kb.reshape(-1, SWA_HEAD_DIM).astype(BF16)
        vb = vb.reshape(-1, SWA_HEAD_DIM).astype(BF16)
        slots = kb.shape[0] // SWA_BAND
        q8, kn, vn = q_ref[0, g], kn_ref[0, g], vn_ref[0, g]
        s = _nt_dot(q8.astype(BF16), kb)
        head = lax.broadcasted_iota(jnp.int32, s.shape, 0)
        slot = lax.broadcasted_iota(jnp.int32, s.shape, 1) & (slots - 1)
        s = jnp.where(slot == head, s, NEG_INF)
        s_n = jnp.sum(q8 * kn, axis=-1, keepdims=True)
        m = jnp.maximum(jnp.max(s, axis=-1, keepdims=True), s_n)
        p, p_n = jnp.exp(s - m), jnp.exp(s_n - m)
        l = jnp.sum(p, axis=-1, keepdims=True) + p_n
        outs.append((_dot(p.astype(BF16), vb) + p_n * vn) / l)
        lses.append(m + jnp.log(l))
    mx = jnp.maximum(jnp.maximum(lses[0], lses[1]), lses[2])
    es = [jnp.exp(x - mx) for x in lses]
    inv = 1.0 / (es[0] + es[1] + es[2])
    for g in range(len(SWA_GROUPS)):
        o_ref[0, g] = outs[g] * (es[g] * inv)


def _swa_sample(q, kn, vn, states):
    db = q.shape[0]
    tok = pl.BlockSpec((1, len(SWA_GROUPS), SUBLANES, SWA_HEAD_DIM), lambda b: (b, 0, 0, 0))
    views, specs = [], []
    for g, (window, dil) in enumerate(SWA_GROUPS):
        for buf in states[g]:
            assert buf.shape[1:] == (window, HEADS_PER_GROUP, SWA_HEAD_DIM) and window == SWA_BAND * dil
            if dil == 1:
                views.append(buf.reshape(db * SWA_BAND * HEADS_PER_GROUP, SWA_HEAD_DIM))
                specs.append(pl.BlockSpec((SWA_BAND * HEADS_PER_GROUP, SWA_HEAD_DIM), lambda b: (b, 0)))
            else:
                views.append(buf.reshape(db * SWA_BAND, dil * HEADS_PER_GROUP, SWA_HEAD_DIM))
                specs.append(pl.BlockSpec((SWA_BAND, SUBLANES, SWA_HEAD_DIM), lambda b: (b, 0, 0)))
    return pl.pallas_call(
        _swa_sample_kernel,
        grid=(db,),
        in_specs=[tok, tok, tok] + specs,
        out_specs=tok,
        out_shape=jax.ShapeDtypeStruct(q.shape, F32),
        compiler_params=_params("arbitrary"),
        name="swa_sample",
    )(q, kn, vn, *views)


def kernel(x_prompt, x_sample, c_prompt, c_sample, page_table, cache_mla_latent, cache_mla_krope, state_swa_k0, state_swa_v0, state_swa_k1, state_swa_v1, state_swa_k2, state_swa_v2, g_norm_mix, g_norm_ff, w_ada, b_ada, w_ff1, w_ff2, w_mla_in, g_mla_qa, g_mla_kva, w_mla_uq, w_mla_uk, w_mla_uv, g_mla_qn, g_mla_qr, g_mla_kn, g_mla_kr, w_mla_o, g_kv_norm, w_ada_kv, b_ada_kv, w_kv, g_swa_k, w_swa_q, g_swa_q, w_swa_o):
    nb, t, d = x_prompt.shape
    db, s_len, _ = x_sample.shape
    assert s_len == 1 and w_mla_in.shape[0] == 1 and w_swa_q.shape[0] == 1 and g_norm_mix.shape[0] == 2
    assert db % SUBLANES == 0 and cache_mla_latent.shape[2] == 1
    n_pages = page_table.shape[1]
    n_phys, psz = cache_mla_latent.shape[:2]
    tm = min(512, t)
    tq = min(1024, t)
    chunk = DECODE_CHUNK if n_pages % DECODE_CHUNK == 0 else 1
    pp = DECODE_PAGES if n_pages % DECODE_PAGES == 0 else chunk
    dff = w_ff1.shape[-1]
    tf = min(1024, dff)

    n_mod = -(-(db + nb) // SUBLANES) * SUBLANES
    c_all = jnp.concatenate([c_sample, c_prompt, jnp.zeros((n_mod - db - nb, d), F32)], axis=0)
    prm = _Rows(nb, t, tm, lambda n: (n_mod, 1, n), lambda w: (1, 1, w), lambda b, k: (db + b, 0, k))
    smp = _Rows(1, db, db, lambda n: (1, n_mod, n), lambda w: (1, db, w), lambda b, k: (0, 0, k))

    bf = lambda w: w.astype(BF16)
    pad_lanes = lambda w, n: jnp.pad(w, [(0, 0)] * (w.ndim - 1) + [(0, n - w.shape[-1])])
    row = lambda v: v.reshape(1, -1)

    pos_p = jnp.arange(t, dtype=F32)
    pos_s = jnp.full((db,), PAST_LEN, F32)
    rope_mla = {prm: _rope_tables(pos_p, ROPE_DIM), smp: _rope_tables(pos_s, ROPE_DIM)}
    rope_swa = {prm: _rope_tables(pos_p, ROT_DIM), smp: _rope_tables(pos_s, ROT_DIM)}

    xs = {prm: x_prompt, smp: x_sample.reshape(1, db, d)}

    mod0 = _ada(c_all, w_ada, b_ada, 0)
    w_in = bf(pad_lanes(w_mla_in[0], Q_LORA + KV_LORA + LANES))
    g_kr = pad_lanes(row(g_mla_kr[0]), LANES)
    g_qr = pad_lanes(row(g_mla_qr[0]), LANES)
    w_uq = w_mla_uq[0]
    w_uq_n = bf(w_uq[:, :, :NOPE_DIM].reshape(Q_LORA, MLA_W))
    w_uq_r = bf(pad_lanes(w_uq[:, :, NOPE_DIM:], LANES).reshape(Q_LORA, MLA_W))
    w_uk = bf(w_mla_uk[0].reshape(KV_LORA, MLA_W))
    w_uv = bf(w_mla_uv[0].reshape(KV_LORA, MLA_W))
    w_uk_t3 = bf(jnp.transpose(w_mla_uk[0], (1, 2, 0)))
    w_uv_3 = bf(jnp.transpose(w_mla_uv[0], (1, 0, 2)))
    w_o = bf(w_mla_o[0].reshape(MLA_W, d))
    w1, w2 = bf(w_ff1), bf(w_ff2)

    proj = {}
    for rows in (prm, smp):
        cq, lat, latb, kr, krb = _mla_in(rows, xs[rows], row(g_norm_mix[0]), mod0, w_in, row(g_mla_qa[0]),
                                         row(g_mla_kva[0]), g_kr, rope_mla[rows])
        q = _q_up(rows, cq, w_uq_n, w_uq_r, row(g_mla_qn[0]), g_qr, rope_mla[rows])
        kc, v = _kv_up(rows, latb, krb, w_uk, w_uv, row(g_mla_kn[0]))
        proj[rows] = (q, kc, v, lat, kr)

    q, kc, v, lat_p, kr_p = proj[prm]
    mix_p = _flash(q, kc, v, tq)

    q, kc, _, lat_s, kr_s = proj[smp]
    wq = jnp.transpose(_absorb(q, row(g_mla_kn[0]), w_uk_t3), (1, 0, 2))
    q_b = jnp.transpose(q[0], (1, 0, 2)).astype(F32)
    kc_b = jnp.transpose(kc[0], (1, 0, 2)).astype(F32)
    lat_rows = cache_mla_latent.reshape(n_phys * psz * LAT_SPLIT, LANES)
    kr_t = jnp.transpose(cache_mla_krope, (0, 2, 3, 1)).reshape(n_phys, ROPE_DIM, psz)
    o_lat = _decode(page_table, lat_rows, kr_t, w_uk_t3.reshape(MLA_W, KV_LORA),
                    wq, q_b, kc_b, lat_s.reshape(db, 1, KV_LORA), pp, chunk)
    mix_s = _uv(jnp.transpose(o_lat, (1, 0, 2)), w_uv_3).reshape(1, db, MLA_W)

    mixes = {prm: mix_p, smp: mix_s}
    for rows in (prm, smp):
        x = _proj_res(rows, mixes[rows], w_o, xs[rows], mod0, 2)
        xs[rows] = _mlp(rows, x, row(g_norm_ff[0]), mod0, w1, w2, 0, tf)

    mod_kv = _ada(c_all, w_ada_kv[None], b_ada_kv[None], 0)
    mod1 = _ada(c_all, w_ada, b_ada, 1)
    w_kv2 = bf(w_kv.reshape(d, 2 * SWA_W))
    w_q = bf(w_swa_q[0].reshape(d, SWA_W))
    w_so = bf(w_swa_o[0].reshape(SWA_W, d))
    ngrp = len(SWA_GROUPS)

    kv_p = _shared_kv(prm, xs[prm], row(g_kv_norm), mod_kv, w_kv2, row(g_swa_k), rope_swa[prm], True)
    q_p = _swa_q(prm, xs[prm], row(g_norm_mix[1]), mod1, w_q, row(g_swa_q[0]), rope_swa[prm], True)
    k_p, v_p, kd, vd, qd = kv_p[0], kv_p[1], kv_p[2:2 + ngrp], kv_p[2 + ngrp:], q_p[1:]
    groups = [_swa_prompt_group(qd[g], kd[g], vd[g], g) for g in range(ngrp)]
    xs[prm] = _swa_proj_res(prm, [o for o, _ in groups], [l for _, l in groups], w_so, xs[prm], mod1, 2)

    k_s, v_s = _shared_kv(smp, xs[smp], row(g_kv_norm), mod_kv, w_kv2, row(g_swa_k), rope_swa[smp], False)
    (q_s,) = _swa_q(smp, xs[smp], row(g_norm_mix[1]), mod1, w_q, row(g_swa_q[0]), rope_swa[smp], False)
    head_rows = lambda x: jnp.pad(x.reshape(db, ngrp, HEADS_PER_GROUP, SWA_HEAD_DIM),
                                  ((0, 0), (0, 0), (0, SUBLANES - HEADS_PER_GROUP), (0, 0)))
    states = ((state_swa_k0, state_swa_v0), (state_swa_k1, state_swa_v1), (state_swa_k2, state_swa_v2))
    o_s = _swa_sample(head_rows(q_s), head_rows(k_s), head_rows(v_s), states)
    o_s = o_s[:, :, :HEADS_PER_GROUP].reshape(1, db, SWA_W).astype(BF16)
    xs[smp] = _proj_res(smp, o_s, w_so, xs[smp], mod1, 2)

    for rows in (prm, smp):
        xs[rows] = _mlp(rows, xs[rows], row(g_norm_ff[1]), mod1, w1, w2, 1, tf)

    def group_heads(x, g, n_rows):
        part = x[:, x.shape[1] - n_rows:, g * GROUP_W:(g + 1) * GROUP_W]
        return part.reshape(x.shape[0], n_rows, HEADS_PER_GROUP, SWA_HEAD_DIM)

    outs = [xs[prm], xs[smp].reshape(db, 1, d),
            lat_p.reshape(nb, t, 1, KV_LORA), kr_p.reshape(nb, t, 1, ROPE_DIM),
            lat_s.reshape(db, 1, 1, KV_LORA), kr_s.reshape(db, 1, 1, ROPE_DIM)]
    for g, (window, _) in enumerate(SWA_GROUPS):
        outs += [group_heads(k_p, g, min(window, t)), group_heads(v_p, g, min(window, t))]
    k_s3, v_s3 = k_s.reshape(db, 1, SWA_W), v_s.reshape(db, 1, SWA_W)
    for g in range(ngrp):
        outs += [group_heads(k_s3, g, 1), group_heads(v_s3, g, 1)]
    return tuple(outs)
```

```python
import functools

import jax
import jax.numpy as jnp
from jax import lax
from jax.experimental import pallas as pl
from jax.experimental.pallas import tpu as pltpu

F32 = jnp.float32
BF16 = jnp.bfloat16

MLA_HEADS = 16
Q_LORA = 512
KV_LORA = 512
NOPE_DIM = 128
ROPE_DIM = 64
V_DIM = 128
SWA_GROUPS = ((128, 1), (512, 4), (2048, 16))
HEADS_PER_GROUP = 4
N_SWA_HEADS = HEADS_PER_GROUP * len(SWA_GROUPS)
SWA_HEAD_DIM = 128
ROT_DIM = SWA_HEAD_DIM // 4
ROPE_THETA = 500000.0
PAST_LEN = 16384
EPS = 1e-6
MLA_SCALE = (NOPE_DIM + ROPE_DIM) ** -0.5
SWA_SCALE = SWA_HEAD_DIM ** -0.5

LANES = 128
SUBLANES = 8
VMEM_LIMIT_BYTES = 56 * 2**20
SWA_BAND = 128
GROUP_W = HEADS_PER_GROUP * SWA_HEAD_DIM
SWA_W = N_SWA_HEADS * SWA_HEAD_DIM
MLA_W = MLA_HEADS * LANES
MLA_QK = 2 * LANES
FLASH_HEADS = 2
DECODE_PAGES = 32
DECODE_CHUNK = 4
LAT_SPLIT = KV_LORA // LANES
NEG_INF = float("-inf")


def _params(*sem):
    return pltpu.CompilerParams(dimension_semantics=sem, vmem_limit_bytes=VMEM_LIMIT_BYTES)


def _nt_dot(a, b):
    return lax.dot_general(a, b, (((1,), (1,)), ((), ())), preferred_element_type=F32)


def _dot(a, b):
    return jnp.dot(a, b, preferred_element_type=F32)


def _rms(x, g, n):
    ms = jnp.sum(x * x, axis=-1, keepdims=True) * (1.0 / n)
    return x * lax.rsqrt(ms + EPS) * g


def _rope(x, cos, sin_lo, sin_hi, half):
    return x * cos + pltpu.roll(x, LANES - half, 1) * sin_lo + pltpu.roll(x, half, 1) * sin_hi


def _norm_mod(x, g, shift, scale):
    return _rms(x, g, x.shape[-1]) * (1.0 + scale) + shift


def _rope_tables(pos, rot_dim):
    half = rot_dim // 2
    inv_freq = ROPE_THETA ** (-(jnp.arange(half, dtype=F32) * 2.0 / rot_dim))
    ang = pos.astype(F32)[:, None] * inv_freq[None, :]
    cos, sin = jnp.cos(ang), jnp.sin(ang)
    t = pos.shape[0]
    zeros = lambda n: jnp.zeros((t, n), F32)
    cos_t = jnp.concatenate([cos, cos, jnp.ones((t, LANES - rot_dim), F32)], axis=1)
    sin_lo = jnp.concatenate([-sin, zeros(LANES - half)], axis=1)
    sin_hi = jnp.concatenate([zeros(half), sin, zeros(LANES - rot_dim)], axis=1)
    return cos_t, sin_lo, sin_hi


class _Rows:
    def __init__(self, nb, t, tm, mod_view, mod_block, mod_index):
        self.nb, self.t, self.tm = nb, t, tm
        self._mod_view, self._mod_block, self._mod_index = mod_view, mod_block, mod_index

    def mod_arr(self, mod):
        return mod.reshape(self._mod_view(mod.shape[1]))

    def mod_spec(self, k, d):
        idx = self._mod_index
        return pl.BlockSpec(self._mod_block(d), lambda b, *_: idx(b, k))

    def row_spec(self, width):
        return pl.BlockSpec((1, self.tm, width), lambda b, i, *_: (b, i, 0))

    def tab_spec(self):
        return pl.BlockSpec((self.tm, LANES), lambda b, i, *_: (i, 0))


def _full(shape):
    return pl.BlockSpec(shape, lambda *_: (0,) * len(shape))


def _ada_kernel(c_ref, w_ref, b_ref, o_ref):
    c = c_ref[...]
    s = c / (1.0 + jnp.exp(-c))
    o_ref[...] = _dot(s.astype(BF16), w_ref[...].astype(BF16)) + b_ref[...]


def _ada(c, w, b, layer, tn=1024):
    m, d = c.shape
    nl, _, n = w.shape
    return pl.pallas_call(
        _ada_kernel,
        grid=(n // tn,),
        in_specs=[pl.BlockSpec((m, d), lambda j: (0, 0)),
                  pl.BlockSpec((None, d, tn), lambda j: (layer, 0, j)),
                  pl.BlockSpec((None, 1, tn), lambda j: (layer, 0, j))],
        out_specs=pl.BlockSpec((m, tn), lambda j: (0, j)),
        out_shape=jax.ShapeDtypeStruct((m, n), F32),
        compiler_params=_params("arbitrary"),
        name="ada_mod",
    )(c, w, b.reshape(nl, 1, n))


def _mla_in_kernel(x_ref, g_ref, sh_ref, sc_ref, w_ref, gqa_ref, gkva_ref, gkr_ref,
                   cos_ref, slo_ref, shi_ref, cq_ref, lat_ref, latb_ref, kr_ref, krb_ref):
    xn = _norm_mod(x_ref[0], g_ref[...], sh_ref[0], sc_ref[0])
    a = _dot(xn.astype(BF16), w_ref[...])
    cq_ref[0] = _rms(a[:, :Q_LORA], gqa_ref[...], Q_LORA).astype(BF16)
    lat = _rms(a[:, Q_LORA:Q_LORA + KV_LORA], gkva_ref[...], KV_LORA)
    lat_ref[0] = lat
    latb_ref[0] = lat.astype(BF16)
    kr = _rms(a[:, Q_LORA + KV_LORA:], gkr_ref[...], ROPE_DIM)
    kr = _rope(kr, cos_ref[...], slo_ref[...], shi_ref[...], ROPE_DIM // 2)
    kr_ref[0] = kr[:, :ROPE_DIM]
    krb_ref[0] = kr.astype(BF16)


def _mla_in(rows, x, g, mod, w, gqa, gkva, gkr, rope):
    d = x.shape[-1]
    n = w.shape[1]
    marr = rows.mod_arr(mod)
    shape = lambda width, dt: jax.ShapeDtypeStruct((rows.nb, rows.t, width), dt)
    return pl.pallas_call(
        _mla_in_kernel,
        grid=(rows.nb, rows.t // rows.tm),
        in_specs=[rows.row_spec(d), _full((1, d)), rows.mod_spec(0, d), rows.mod_spec(1, d),
                  _full((d, n)), _full((1, Q_LORA)), _full((1, KV_LORA)), _full((1, LANES)),
                  rows.tab_spec(), rows.tab_spec(), rows.tab_spec()],
        out_specs=[rows.row_spec(Q_LORA), rows.row_spec(KV_LORA), rows.row_spec(KV_LORA),
                   rows.row_spec(ROPE_DIM), rows.row_spec(LANES)],
        out_shape=[shape(Q_LORA, BF16), shape(KV_LORA, F32), shape(KV_LORA, BF16),
                   shape(ROPE_DIM, F32), shape(LANES, BF16)],
        compiler_params=_params("arbitrary", "arbitrary"),
        name="mla_in",
    )(x, g, marr, marr, w, gqa, gkva, gkr, *rope)


def _q_up_kernel(cq_ref, wn_ref, wr_ref, gqn_ref, gqr_ref, cos_ref, slo_ref, shi_ref, q_ref):
    cq = cq_ref[0]
    an = _dot(cq, wn_ref[...])
    ar = _dot(cq, wr_ref[...])
    cos, slo, shi = cos_ref[...], slo_ref[...], shi_ref[...]
    for h in range(MLA_HEADS):
        sl = slice(h * LANES, (h + 1) * LANES)
        qn = _rms(an[:, sl], gqn_ref[...], NOPE_DIM) * MLA_SCALE
        qr = _rope(_rms(ar[:, sl], gqr_ref[...], ROPE_DIM), cos, slo, shi, ROPE_DIM // 2) * MLA_SCALE
        q_ref[0, h, :, 0:LANES] = qn.astype(BF16)
        q_ref[0, h, :, LANES:MLA_QK] = qr.astype(BF16)


def _q_up(rows, cq, wn, wr, gqn, gqr, rope):
    tm = rows.tm
    return pl.pallas_call(
        _q_up_kernel,
        grid=(rows.nb, rows.t // tm),
        in_specs=[rows.row_spec(Q_LORA), _full((Q_LORA, MLA_W)), _full((Q_LORA, MLA_W)),
                  _full((1, LANES)), _full((1, LANES)),
                  rows.tab_spec(), rows.tab_spec(), rows.tab_spec()],
        out_specs=pl.BlockSpec((1, MLA_HEADS, tm, MLA_QK), lambda b, i: (b, 0, i, 0)),
        out_shape=jax.ShapeDtypeStruct((rows.nb, MLA_HEADS, rows.t, MLA_QK), BF16),
        compiler_params=_params("arbitrary", "arbitrary"),
        name="mla_q_up",
    )(cq, wn, wr, gqn, gqr, *rope)


def _kv_up_kernel(latb_ref, krb_ref, wk_ref, wv_ref, gkn_ref, k_ref, v_ref):
    lat = latb_ref[0]
    ak = _dot(lat, wk_ref[...])
    av = _dot(lat, wv_ref[...])
    krb = krb_ref[0]
    for h in range(MLA_HEADS):
        sl = slice(h * LANES, (h + 1) * LANES)
        k_ref[0, h, :, 0:LANES] = _rms(ak[:, sl], gkn_ref[...], NOPE_DIM).astype(BF16)
        k_ref[0, h, :, LANES:MLA_QK] = krb
        v_ref[0, h] = av[:, sl].astype(BF16)


def _kv_up(rows, latb, krb, wk, wv, gkn):
    tm = rows.tm
    hspec = lambda w: pl.BlockSpec((1, MLA_HEADS, tm, w), lambda b, i: (b, 0, i, 0))
    return pl.pallas_call(
        _kv_up_kernel,
        grid=(rows.nb, rows.t // tm),
        in_specs=[rows.row_spec(KV_LORA), rows.row_spec(LANES), _full((KV_LORA, MLA_W)),
                  _full((KV_LORA, MLA_W)), _full((1, LANES))],
        out_specs=[hspec(MLA_QK), hspec(V_DIM)],
        out_shape=[jax.ShapeDtypeStruct((rows.nb, MLA_HEADS, rows.t, MLA_QK), BF16),
                   jax.ShapeDtypeStruct((rows.nb, MLA_HEADS, rows.t, V_DIM), BF16)],
        compiler_params=_params("arbitrary", "arbitrary"),
        name="mla_kv_up",
    )(latb, krb, wk, wv, gkn)


def _flash_kernel(qi_ref, ki_ref, q_ref, k_ref, v_ref, o_ref, m_ref, l_ref, acc_ref):
    s_idx = pl.program_id(2)
    qi, ki = qi_ref[s_idx], ki_ref[s_idx]

    @pl.when(ki == 0)
    def _():
        m_ref[...] = jnp.full(m_ref.shape, NEG_INF, F32)
        l_ref[...] = jnp.zeros(l_ref.shape, F32)
        acc_ref[...] = jnp.zeros(acc_ref.shape, F32)

    def update(diagonal):
        for h in range(FLASH_HEADS):
            s = _nt_dot(q_ref[0, h], k_ref[0, h])
            if diagonal:
                rows = lax.broadcasted_iota(jnp.int32, s.shape, 0)
                cols = lax.broadcasted_iota(jnp.int32, s.shape, 1)
                s = jnp.where(cols <= rows, s, NEG_INF)
            m_prev = m_ref[h]
            m_new = jnp.maximum(m_prev, jnp.max(s, axis=-1, keepdims=True))
            corr = jnp.exp(m_prev - m_new)
            p = jnp.exp(s - m_new)
            l_ref[h] = corr * l_ref[h] + jnp.sum(p, axis=-1, keepdims=True)
            acc_ref[h] = corr * acc_ref[h] + _dot(p.astype(BF16), v_ref[0, h])
            m_ref[h] = m_new

    @pl.when(ki < qi)
    def _():
        update(False)

    @pl.when(ki == qi)
    def _():
        update(True)
        for h in range(FLASH_HEADS):
            o_ref[0, :, h * V_DIM:(h + 1) * V_DIM] = (acc_ref[h] / l_ref[h]).astype(o_ref.dtype)


def _flash(q, k, v, tq):
    nb, nh, t, _ = q.shape
    nq = t // tq
    fh = FLASH_HEADS
    pairs = [(i, j) for i in range(nq) for j in range(i + 1)]
    qi = jnp.asarray([p[0] for p in pairs], jnp.int32)
    ki = jnp.asarray([p[1] for p in pairs], jnp.int32)
    grid_spec = pltpu.PrefetchScalarGridSpec(
        num_scalar_prefetch=2,
        grid=(nb, nh // fh, len(pairs)),
        in_specs=[pl.BlockSpec((1, fh, tq, MLA_QK), lambda b, h, s, qi, ki: (b, h, qi[s], 0)),
                  pl.BlockSpec((1, fh, tq, MLA_QK), lambda b, h, s, qi, ki: (b, h, ki[s], 0)),
                  pl.BlockSpec((1, fh, tq, V_DIM), lambda b, h, s, qi, ki: (b, h, ki[s], 0))],
        out_specs=pl.BlockSpec((1, tq, fh * V_DIM), lambda b, h, s, qi, ki: (b, qi[s], h)),
        scratch_shapes=[pltpu.VMEM((fh, tq, 1), F32), pltpu.VMEM((fh, tq, 1), F32),
                        pltpu.VMEM((fh, tq, V_DIM), F32)],
    )
    return pl.pallas_call(
        _flash_kernel,
        grid_spec=grid_spec,
        out_shape=jax.ShapeDtypeStruct((nb, t, nh * V_DIM), BF16),
        compiler_params=_params("arbitrary", "arbitrary", "arbitrary"),
        name="mla_flash",
    )(qi, ki, q, k, v)


def _proj_res_kernel(a_ref, w_ref, x_ref, gate_ref, o_ref):
    o_ref[0] = x_ref[0] + gate_ref[0] * _dot(a_ref[0], w_ref[...])


def _proj_res(rows, a, w, x, mod, gate_k):
    kdim, d = w.shape
    return pl.pallas_call(
        _proj_res_kernel,
        grid=(rows.nb, rows.t // rows.tm),
        in_specs=[rows.row_spec(kdim), _full((kdim, d)), rows.row_spec(d), rows.mod_spec(gate_k, d)],
        out_specs=rows.row_spec(d),
        out_shape=jax.ShapeDtypeStruct((rows.nb, rows.t, d), F32),
        compiler_params=_params("arbitrary", "arbitrary"),
        name="proj_residual",
    )(a, w, x, rows.mod_arr(mod))


def _dilated_rows(r, tm, dil):
    return pl.ds(r, tm // dil, stride=dil) if dil > 1 else pl.ds(0, tm)


def _dilated_spec(tm, dil):
    return pl.BlockSpec((1, dil, tm // dil, GROUP_W), lambda b, i, *_: (b, 0, i, 0))


def _swa_proj_res_kernel(o0_ref, o1_ref, o2_ref, l0_ref, l1_ref, l2_ref, w_ref, x_ref, gate_ref, out_ref,
                         a_scr, lse_scr):
    tm = a_scr.shape[1]
    for g, (o_ref, l_ref) in enumerate(((o0_ref, l0_ref), (o1_ref, l1_ref), (o2_ref, l2_ref))):
        dil = SWA_GROUPS[g][1]
        for r in range(dil):
            for j in range(HEADS_PER_GROUP):
                sl = slice(j * SWA_HEAD_DIM, (j + 1) * SWA_HEAD_DIM)
                a_scr[g * HEADS_PER_GROUP + j, _dilated_rows(r, tm, dil), :] = o_ref[0, r][:, sl].astype(F32)
                lse_scr[g * HEADS_PER_GROUP + j, _dilated_rows(r, tm, dil), :] = l_ref[0, r][:, sl]
    mixed = [None] * N_SWA_HEADS
    for j in range(HEADS_PER_GROUP):
        heads = [g * HEADS_PER_GROUP + j for g in range(len(SWA_GROUPS))]
        lses = [lse_scr[h] for h in heads]
        mx = jnp.maximum(jnp.maximum(lses[0], lses[1]), lses[2])
        es = [jnp.exp(x - mx) for x in lses]
        inv = 1.0 / (es[0] + es[1] + es[2])
        for h, e in zip(heads, es):
            mixed[h] = (a_scr[h] * (e * inv)).astype(BF16)
    out_ref[0] = x_ref[0] + gate_ref[0] * _dot(jnp.concatenate(mixed, axis=1), w_ref[...])


def _swa_proj_res(rows, os_, lses, w, x, mod, gate_k):
    kdim, d = w.shape
    tm = rows.tm
    dspecs = [_dilated_spec(tm, dil) for _, dil in SWA_GROUPS]
    return pl.pallas_call(
        _swa_proj_res_kernel,
        grid=(rows.nb, rows.t // tm),
        in_specs=dspecs + dspecs + [_full((kdim, d)), rows.row_spec(d), rows.mod_spec(gate_k, d)],
        out_specs=rows.row_spec(d),
        out_shape=jax.ShapeDtypeStruct((rows.nb, rows.t, d), F32),
        scratch_shapes=[pltpu.VMEM((N_SWA_HEADS, tm, SWA_HEAD_DIM), F32)] * 2,
        compiler_params=_params("arbitrary", "arbitrary"),
        name="swa_proj_residual",
    )(*os_, *lses, w, x, rows.mod_arr(mod))


def _mlp_kernel(x_ref, g_ref, sh_ref, sc_ref, gate_ref, w1_ref, w2_ref, o_ref, xn_ref, acc_ref):
    f = pl.program_id(2)

    @pl.when(f == 0)
    def _():
        xn_ref[...] = _norm_mod(x_ref[0], g_ref[...], sh_ref[0], sc_ref[0]).astype(BF16)
        acc_ref[...] = jnp.zeros(acc_ref.shape, F32)

    h = jnp.maximum(_dot(xn_ref[...], w1_ref[...]), 0.0)
    acc_ref[...] += _dot((h * h).astype(BF16), w2_ref[...])

    @pl.when(f == pl.num_programs(2) - 1)
    def _():
        o_ref[0] = x_ref[0] + gate_ref[0] * acc_ref[...]


def _mlp(rows, x, g, mod, w1, w2, layer, tf):
    _, d, dff = w1.shape
    tm = rows.tm
    marr = rows.mod_arr(mod)
    return pl.pallas_call(
        _mlp_kernel,
        grid=(rows.nb, rows.t // tm, dff // tf),
        in_specs=[rows.row_spec(d), _full((1, d)), rows.mod_spec(3, d), rows.mod_spec(4, d),
                  rows.mod_spec(5, d),
                  pl.BlockSpec((None, d, tf), lambda b, i, f: (layer, 0, f)),
                  pl.BlockSpec((None, tf, d), lambda b, i, f: (layer, f, 0))],
        out_specs=rows.row_spec(d),
        out_shape=jax.ShapeDtypeStruct((rows.nb, rows.t, d), F32),
        scratch_shapes=[pltpu.VMEM((tm, d), BF16), pltpu.VMEM((tm, d), F32)],
        compiler_params=_params("arbitrary", "arbitrary", "arbitrary"),
        name="mlp",
    )(x, g, marr, marr, marr, w1, w2)


def _decode_kernel(pp, chunk, pt_ref, *refs):
    lat_refs, kr_refs = refs[:pp], refs[pp:2 * pp]
    (wukt_ref, wq_ref, q_ref, kc_ref, latn_ref, o_ref, m_ref, l_ref, acc_ref) = refs[2 * pp:]
    j = pl.program_id(1)

    @pl.when(j == 0)
    def _():
        m_ref[...] = jnp.full(m_ref.shape, NEG_INF, F32)
        l_ref[...] = jnp.zeros(l_ref.shape, F32)
        acc_ref[...] = jnp.zeros(acc_ref.shape, F32)

    def page(ref):
        n = ref.shape[0] // LAT_SPLIT
        return jnp.concatenate([ref[pl.ds(c, n, stride=LAT_SPLIT), :] for c in range(LAT_SPLIT)], axis=1)

    wq = wq_ref[0]
    qr = q_ref[0][:, LANES:LANES + ROPE_DIM].astype(BF16)
    lats, scores = [], []
    for i in range(0, pp, chunk):
        lat = jnp.concatenate([page(lat_refs[i + c]) for c in range(chunk)], axis=0).astype(BF16)
        kr_t = jnp.concatenate([kr_refs[i + c][0] for c in range(chunk)], axis=1).astype(BF16)
        raw_t = _nt_dot(wukt_ref[...], lat)
        raw3 = raw_t.reshape(MLA_HEADS, NOPE_DIM, raw_t.shape[-1])
        ssq = jnp.sum(raw3 * raw3, axis=1)
        scores.append(_nt_dot(wq, lat) * lax.rsqrt(ssq * (1.0 / NOPE_DIM) + EPS) + _dot(qr, kr_t))
        lats.append(lat)
    s = jnp.concatenate(scores, axis=1)
    m_prev = m_ref[...]
    m_new = jnp.maximum(m_prev, jnp.max(s, axis=-1, keepdims=True))
    corr = jnp.exp(m_prev - m_new)
    p = jnp.exp(s - m_new)
    l_ref[...] = corr * l_ref[...] + jnp.sum(p, axis=-1, keepdims=True)
    acc_ref[...] = corr * acc_ref[...] + _dot(p.astype(BF16), jnp.concatenate(lats, axis=0))
    m_ref[...] = m_new

    @pl.when(j == pl.num_programs(1) - 1)
    def _():
        s_n = jnp.sum(q_ref[0] * kc_ref[0], axis=-1, keepdims=True)
        m_prev = m_ref[...]
        m_new = jnp.maximum(m_prev, s_n)
        corr = jnp.exp(m_prev - m_new)
        p_n = jnp.exp(s_n - m_new)
        l_fin = corr * l_ref[...] + p_n
        acc = corr * acc_ref[...] + p_n * latn_ref[0]
        o_ref[0] = acc / l_fin


def _decode(page_table, cache_lat, cache_kr, wukt, wq, q, kc, latn, pp, chunk):
    db, n_pages = page_table.shape
    psz = cache_kr.shape[2]
    pt = page_table.reshape(-1)
    phys = lambda b, j, pt, i: pt[b * n_pages + j * pp + i]
    lat_spec = lambda i: pl.BlockSpec((psz * LAT_SPLIT, LANES), lambda b, j, pt: (phys(b, j, pt, i), 0))
    kr_spec = lambda i: pl.BlockSpec((1, ROPE_DIM, psz), lambda b, j, pt: (phys(b, j, pt, i), 0, 0))
    per_b = lambda shape: pl.BlockSpec((1,) + shape, lambda b, j, pt: (b, 0, 0))
    grid_spec = pltpu.PrefetchScalarGridSpec(
        num_scalar_prefetch=1,
        grid=(db, n_pages // pp),
        in_specs=([lat_spec(i) for i in range(pp)] + [kr_spec(i) for i in range(pp)]
                  + [pl.BlockSpec(wukt.shape, lambda b, j, pt: (0, 0)),
                     per_b((MLA_HEADS, KV_LORA)), per_b((MLA_HEADS, MLA_QK)), per_b((MLA_HEADS, MLA_QK)),
                     per_b((1, KV_LORA))]),
        out_specs=per_b((MLA_HEADS, KV_LORA)),
        scratch_shapes=[pltpu.VMEM((MLA_HEADS, 1), F32), pltpu.VMEM((MLA_HEADS, 1), F32),
                        pltpu.VMEM((MLA_HEADS, KV_LORA), F32)],
    )
    return pl.pallas_call(
        functools.partial(_decode_kernel, pp, chunk),
        grid_spec=grid_spec,
        out_shape=jax.ShapeDtypeStruct((db, MLA_HEADS, KV_LORA), F32),
        compiler_params=_params("arbitrary", "arbitrary"),
        name="mla_decode",
    )(pt, *([cache_lat] * pp), *([cache_kr] * pp), wukt, wq, q, kc, latn)


def _absorb_kernel(q_ref, gkn_ref, wukt_ref, o_ref):
    qg = (q_ref[0, 0][:, :LANES].astype(F32) * gkn_ref[...]).astype(BF16)
    o_ref[0] = _dot(qg, wukt_ref[0]).astype(BF16)


def _absorb(q, gkn, wukt3):
    _, nh, db, _ = q.shape
    return pl.pallas_call(
        _absorb_kernel,
        grid=(nh,),
        in_specs=[pl.BlockSpec((1, 1, db, MLA_QK), lambda h: (0, h, 0, 0)), _full((1, LANES)),
                  pl.BlockSpec((1, NOPE_DIM, KV_LORA), lambda h: (h, 0, 0))],
        out_specs=pl.BlockSpec((1, db, KV_LORA), lambda h: (h, 0, 0)),
        out_shape=jax.ShapeDtypeStruct((nh, db, KV_LORA), BF16),
        compiler_params=_params("arbitrary"),
        name="mla_absorb_q",
    )(q, gkn, wukt3)


def _uv_kernel(o_ref, w_ref, out_ref):
    out_ref[...] = _dot(o_ref[0].astype(BF16), w_ref[0]).astype(BF16)


def _uv(o_lat, wuv3):
    nh, db, _ = o_lat.shape
    return pl.pallas_call(
        _uv_kernel,
        grid=(nh,),
        in_specs=[pl.BlockSpec((1, db, KV_LORA), lambda h: (h, 0, 0)),
                  pl.BlockSpec((1, KV_LORA, V_DIM), lambda h: (h, 0, 0))],
        out_specs=pl.BlockSpec((db, V_DIM), lambda h: (0, h)),
        out_shape=jax.ShapeDtypeStruct((db, nh * V_DIM), BF16),
        compiler_params=_params("arbitrary"),
        name="mla_uv",
    )(o_lat, wuv3)


def _head_norm_rope(a, g, cos, slo, shi, scale):
    outs = []
    for h in range(a.shape[-1] // LANES):
        y = _rope(_rms(a[:, h * LANES:(h + 1) * LANES], g, SWA_HEAD_DIM), cos, slo, shi, ROT_DIM // 2)
        outs.append(y * scale if scale != 1.0 else y)
    return outs


def _write_dilated(heads, scr_ref, out_refs):
    tm = scr_ref.shape[1]
    for h, y in enumerate(heads):
        scr_ref[h] = y
    for g, (_, dil) in enumerate(SWA_GROUPS):
        for r in range(dil):
            for j in range(HEADS_PER_GROUP):
                rows = scr_ref[g * HEADS_PER_GROUP + j, _dilated_rows(r, tm, dil), :]
                out_refs[g][0, r, :, j * SWA_HEAD_DIM:(j + 1) * SWA_HEAD_DIM] = rows.astype(BF16)


def _dilated_outs(rows):
    specs = [_dilated_spec(rows.tm, dil) for _, dil in SWA_GROUPS]
    shapes = [jax.ShapeDtypeStruct((rows.nb, dil, rows.t // dil, GROUP_W), BF16) for _, dil in SWA_GROUPS]
    return specs, shapes


def _head_scratch(rows, dilated):
    return [pltpu.VMEM((N_SWA_HEADS, rows.tm, SWA_HEAD_DIM), F32)] if dilated else []


def _swa_q_kernel(dilated, x_ref, g_ref, sh_ref, sc_ref, w_ref, gq_ref, cos_ref, slo_ref, shi_ref,
                  q_ref, *rest):
    xn = _norm_mod(x_ref[0], g_ref[...], sh_ref[0], sc_ref[0])
    a = _dot(xn.astype(BF16), w_ref[...])
    heads = _head_norm_rope(a, gq_ref[...], cos_ref[...], slo_ref[...], shi_ref[...], SWA_SCALE)
    q_ref[0] = jnp.concatenate(heads, axis=1)
    if dilated:
        _write_dilated(heads, rest[-1], rest[:-1])


def _swa_q(rows, x, g, mod, w, gq, rope, dilated):
    d, n = w.shape
    marr = rows.mod_arr(mod)
    dspecs, dshapes = _dilated_outs(rows) if dilated else ([], [])
    return pl.pallas_call(
        functools.partial(_swa_q_kernel, dilated),
        grid=(rows.nb, rows.t // rows.tm),
        in_specs=[rows.row_spec(d), _full((1, d)), rows.mod_spec(0, d), rows.mod_spec(1, d),
                  _full((d, n)), _full((1, LANES)), rows.tab_spec(), rows.tab_spec(), rows.tab_spec()],
        out_specs=[rows.row_spec(n)] + dspecs,
        out_shape=[jax.ShapeDtypeStruct((rows.nb, rows.t, n), F32)] + dshapes,
        scratch_shapes=_head_scratch(rows, dilated),
        compiler_params=_params("arbitrary", "arbitrary"),
        name="swa_q",
    )(x, g, marr, marr, w, gq, *rope)


def _shared_kv_kernel(dilated, x_ref, g_ref, sh_ref, sc_ref, w_ref, gk_ref, cos_ref, slo_ref, shi_ref,
                      k_ref, v_ref, *rest):
    xn_ref = rest[2 * len(SWA_GROUPS) if dilated else 0]
    c = pl.program_id(2)
    ngrp = len(SWA_GROUPS)

    @pl.when(c == 0)
    def _():
        xn_ref[...] = _norm_mod(x_ref[0], g_ref[...], sh_ref[0], sc_ref[0]).astype(BF16)
        a = _dot(xn_ref[...], w_ref[...])
        heads = _head_norm_rope(a, gk_ref[...], cos_ref[...], slo_ref[...], shi_ref[...], 1.0)
        k_ref[0] = jnp.concatenate(heads, axis=1)
        if dilated:
            _write_dilated(heads, rest[-1], rest[:ngrp])

    @pl.when(c == 1)
    def _():
        v = _dot(xn_ref[...], w_ref[...])
        v_ref[0] = v
        if dilated:
            heads = [v[:, h * SWA_HEAD_DIM:(h + 1) * SWA_HEAD_DIM] for h in range(N_SWA_HEADS)]
            _write_dilated(heads, rest[-1], rest[ngrp:2 * ngrp])


def _shared_kv(rows, x, g, mod, w, gk, rope, dilated):
    d = x.shape[-1]
    marr = rows.mod_arr(mod)
    dspecs, dshapes = _dilated_outs(rows) if dilated else ([], [])
    return pl.pallas_call(
        functools.partial(_shared_kv_kernel, dilated),
        grid=(rows.nb, rows.t // rows.tm, 2),
        in_specs=[rows.row_spec(d), _full((1, d)), rows.mod_spec(0, d), rows.mod_spec(1, d),
                  pl.BlockSpec((d, SWA_W), lambda b, i, c: (0, c)), _full((1, LANES)),
                  rows.tab_spec(), rows.tab_spec(), rows.tab_spec()],
        out_specs=[rows.row_spec(SWA_W)] * 2 + dspecs + dspecs,
        out_shape=[jax.ShapeDtypeStruct((rows.nb, rows.t, SWA_W), F32)] * 2 + dshapes + dshapes,
        scratch_shapes=[pltpu.VMEM((rows.tm, d), BF16)] + _head_scratch(rows, dilated),
        compiler_params=_params("arbitrary", "arbitrary", "arbitrary"),
        name="shared_kv",
    )(x, g, marr, marr, w, gk, *rope)


def _swa_prompt_kernel(q_ref, kp_ref, kc_ref, vp_ref, vc_ref, o_ref, lse_ref):
    n = pl.program_id(2)
    band = SWA_BAND
    nsub = q_ref.shape[2] // band
    qi = lax.broadcasted_iota(jnp.int32, (band, 2 * band), 0)
    ki = lax.broadcasted_iota(jnp.int32, (band, 2 * band), 1)
    lag = qi + band - ki
    in_band = (lag >= 0) & (lag <= band)
    first_key = jnp.where(n > 0, 0, band)
    for h in range(HEADS_PER_GROUP):
        sl = slice(h * SWA_HEAD_DIM, (h + 1) * SWA_HEAD_DIM)
        keys = jnp.concatenate([kp_ref[0, 0][:, sl], kc_ref[0, 0][:, sl]], axis=0)
        vals = jnp.concatenate([vp_ref[0, 0][:, sl], vc_ref[0, 0][:, sl]], axis=0)
        for i in range(nsub):
            rows = slice(i * band, (i + 1) * band)
            ok = (in_band & (ki >= first_key)) if i == 0 else in_band
            s = _nt_dot(q_ref[0, 0, rows, sl], keys[i * band:(i + 2) * band])
            s = jnp.where(ok, s, NEG_INF)
            m = jnp.max(s, axis=-1, keepdims=True)
            p = jnp.exp(s - m)
            l = jnp.sum(p, axis=-1, keepdims=True)
            o = _dot(p.astype(BF16), vals[i * band:(i + 2) * band]) / l
            o_ref[0, 0, rows, sl] = o.astype(BF16)
            lse_ref[0, 0, rows, sl] = jnp.broadcast_to(m + jnp.log(l), (band, SWA_HEAD_DIM))


def _swa_prompt_group(q, k, v, g, rows_per_step=512):
    nb, dil, m_len, _ = q.shape
    assert SWA_GROUPS[g] == (SWA_BAND * dil, dil) and m_len % SWA_BAND == 0
    rows = min(rows_per_step, m_len)
    nsub = rows // SWA_BAND
    cur = pl.BlockSpec((1, 1, rows, GROUP_W), lambda b, r, n: (b, r, n, 0))
    prev = pl.BlockSpec((1, 1, SWA_BAND, GROUP_W), lambda b, r, n: (b, r, jnp.maximum(n * nsub - 1, 0), 0))
    return pl.pallas_call(
        _swa_prompt_kernel,
        grid=(nb, dil, m_len // rows),
        in_specs=[cur, prev, cur, prev, cur],
        out_specs=[cur, cur],
        out_shape=[jax.ShapeDtypeStruct(q.shape, BF16), jax.ShapeDtypeStruct(q.shape, F32)],
        compiler_params=_params("arbitrary", "arbitrary", "arbitrary"),
        name=f"swa_prompt_g{g}",
    )(q, k, k, v, v)


def _swa_sample_kernel(q_ref, kn_ref, vn_ref, k0_ref, v0_ref, k1_ref, v1_ref, k2_ref, v2_ref, o_ref):
    state = ((k0_ref, v0_ref), (k1_ref, v1_ref), (k2_ref, v2_ref))
    outs, lses = [], []
    for g in range(len(SWA_GROUPS)):
        kb, vb = state[g][0][...], state[g][1][...]
        kb = kb.reshape(-1, SWA_HEAD_DIM).astype(BF16)
        vb = vb.reshape(-1, SWA_HEAD_DIM).astype(BF16)
        slots = kb.shape[0] // SWA_BAND
        q8, kn, vn = q_ref[0, g], kn_ref[0, g], vn_ref[0, g]
        s = _nt_dot(q8.astype(BF16), kb)
        head = lax.broadcasted_iota(jnp.int32, s.shape, 0)
        slot = lax.broadcasted_iota(jnp.int32, s.shape, 1) & (slots - 1)
        s = jnp.where(slot == head, s, NEG_INF)
        s_n = jnp.sum(q8 * kn, axis=-1, keepdims=True)
        m = jnp.maximum(jnp.max(s, axis=-1, keepdims=True), s_n)
        p, p_n = jnp.exp(s - m), jnp.exp(s_n - m)
        l = jnp.sum(p, axis=-1, keepdims=True) + p_n
        outs.append((_dot(p.astype(BF16), vb) + p_n * vn) / l)
        lses.append(m + jnp.log(l))
    mx = jnp.maximum(jnp.maximum(lses[0], lses[1]), lses[2])
    es = [jnp.exp(x - mx) for x in lses]
    inv = 1.0 / (es[0] + es[1] + es[2])
    for g in range(len(SWA_GROUPS)):
        o_ref[0, g] = outs[g] * (es[g] * inv)


def _swa_sample(q, kn, vn, states):
    db = q.shape[0]
    tok = pl.BlockSpec((1, len(SWA_GROUPS), SUBLANES, SWA_HEAD_DIM), lambda b: (b, 0, 0, 0))
    views, specs = [], []
    for g, (window, dil) in enumerate(SWA_GROUPS):
        for buf in states[g]:
            assert buf.shape[1:] == (window, HEADS_PER_GROUP, SWA_HEAD_DIM) and window == SWA_BAND * dil
            if dil == 1:
                views.append(buf.reshape(db * SWA_BAND * HEADS_PER_GROUP, SWA_HEAD_DIM))
                specs.append(pl.BlockSpec((SWA_BAND * HEADS_PER_GROUP, SWA_HEAD_DIM), lambda b: (b, 0)))
            else:
                views.append(buf.reshape(db * SWA_BAND, dil * HEADS_PER_GROUP, SWA_HEAD_DIM))
                specs.append(pl.BlockSpec((SWA_BAND, SUBLANES, SWA_HEAD_DIM), lambda b: (b, 0, 0)))
    return pl.pallas_call(
        _swa_sample_kernel,
        grid=(db,),
        in_specs=[tok, tok, tok] + specs,
        out_specs=tok,
        out_shape=jax.ShapeDtypeStruct(q.shape, F32),
        compiler_params=_params("arbitrary"),
        name="swa_sample",
    )(q, kn, vn, *views)


def kernel(x_prompt, x_sample, c_prompt, c_sample, page_table, cache_mla_latent, cache_mla_krope, state_swa_k0, state_swa_v0, state_swa_k1, state_swa_v1, state_swa_k2, state_swa_v2, g_norm_mix, g_norm_ff, w_ada, b_ada, w_ff1, w_ff2, w_mla_in, g_mla_qa, g_mla_kva, w_mla_uq, w_mla_uk, w_mla_uv, g_mla_qn, g_mla_qr, g_mla_kn, g_mla_kr, w_mla_o, g_kv_norm, w_ada_kv, b_ada_kv, w_kv, g_swa_k, w_swa_q, g_swa_q, w_swa_o):
    nb, t, d = x_prompt.shape
    db, s_len, _ = x_sample.shape
    assert s_len == 1 and w_mla_in.shape[0] == 1 and w_swa_q.shape[0] == 1 and g_norm_mix.shape[0] == 2
    assert db % SUBLANES == 0 and cache_mla_latent.shape[2] == 1
    n_pages = page_table.shape[1]
    n_phys, psz = cache_mla_latent.shape[:2]
    tm = min(512, t)
    tq = min(1024, t)
    chunk = DECODE_CHUNK if n_pages % DECODE_CHUNK == 0 else 1
    pp = DECODE_PAGES if n_pages % DECODE_PAGES == 0 else chunk
    dff = w_ff1.shape[-1]
    tf = min(1024, dff)

    n_mod = -(-(db + nb) // SUBLANES) * SUBLANES
    c_all = jnp.concatenate([c_sample, c_prompt, jnp.zeros((n_mod - db - nb, d), F32)], axis=0)
    prm = _Rows(nb, t, tm, lambda n: (n_mod, 1, n), lambda w: (1, 1, w), lambda b, k: (db + b, 0, k))
    smp = _Rows(1, db, db, lambda n: (1, n_mod, n), lambda w: (1, db, w), lambda b, k: (0, 0, k))

    bf = lambda w: w.astype(BF16)
    pad_lanes = lambda w, n: jnp.pad(w, [(0, 0)] * (w.ndim - 1) + [(0, n - w.shape[-1])])
    row = lambda v: v.reshape(1, -1)

    pos_p = jnp.arange(t, dtype=F32)
    pos_s = jnp.full((db,), PAST_LEN, F32)
    rope_mla = {prm: _rope_tables(pos_p, ROPE_DIM), smp: _rope_tables(pos_s, ROPE_DIM)}
    rope_swa = {prm: _rope_tables(pos_p, ROT_DIM), smp: _rope_tables(pos_s, ROT_DIM)}

    xs = {prm: x_prompt, smp: x_sample.reshape(1, db, d)}

    mod0 = _ada(c_all, w_ada, b_ada, 0)
    w_in = bf(pad_lanes(w_mla_in[0], Q_LORA + KV_LORA + LANES))
    g_kr = pad_lanes(row(g_mla_kr[0]), LANES)
    g_qr = pad_lanes(row(g_mla_qr[0]), LANES)
    w_uq = w_mla_uq[0]
    w_uq_n = bf(w_uq[:, :, :NOPE_DIM].reshape(Q_LORA, MLA_W))
    w_uq_r = bf(pad_lanes(w_uq[:, :, NOPE_DIM:], LANES).reshape(Q_LORA, MLA_W))
    w_uk = bf(w_mla_uk[0].reshape(KV_LORA, MLA_W))
    w_uv = bf(w_mla_uv[0].reshape(KV_LORA, MLA_W))
    w_uk_t3 = bf(jnp.transpose(w_mla_uk[0], (1, 2, 0)))
    w_uv_3 = bf(jnp.transpose(w_mla_uv[0], (1, 0, 2)))
    w_o = bf(w_mla_o[0].reshape(MLA_W, d))
    w1, w2 = bf(w_ff1), bf(w_ff2)

    proj = {}
    for rows in (prm, smp):
        cq, lat, latb, kr, krb = _mla_in(rows, xs[rows], row(g_norm_mix[0]), mod0, w_in, row(g_mla_qa[0]),
                                         row(g_mla_kva[0]), g_kr, rope_mla[rows])
        q = _q_up(rows, cq, w_uq_n, w_uq_r, row(g_mla_qn[0]), g_qr, rope_mla[rows])
        kc, v = _kv_up(rows, latb, krb, w_uk, w_uv, row(g_mla_kn[0]))
        proj[rows] = (q, kc, v, lat, kr)

    q, kc, v, lat_p, kr_p = proj[prm]
    mix_p = _flash(q, kc, v, tq)

    q, kc, _, lat_s, kr_s = proj[smp]
    wq = jnp.transpose(_absorb(q, row(g_mla_kn[0]), w_uk_t3), (1, 0, 2))
    q_b = jnp.transpose(q[0], (1, 0, 2)).astype(F32)
    kc_b = jnp.transpose(kc[0], (1, 0, 2)).astype(F32)
    lat_rows = cache_mla_latent.reshape(n_phys * psz * LAT_SPLIT, LANES)
    kr_t = jnp.transpose(cache_mla_krope, (0, 2, 3, 1)).reshape(n_phys, ROPE_DIM, psz)
    o_lat = _decode(page_table, lat_rows, kr_t, w_uk_t3.reshape(MLA_W, KV_LORA),
                    wq, q_b, kc_b, lat_s.reshape(db, 1, KV_LORA), pp, chunk)
    mix_s = _uv(jnp.transpose(o_lat, (1, 0, 2)), w_uv_3).reshape(1, db, MLA_W)

    mixes = {prm: mix_p, smp: mix_s}
    for rows in (prm, smp):
        x = _proj_res(rows, mixes[rows], w_o, xs[rows], mod0, 2)
        xs[rows] = _mlp(rows, x, row(g_norm_ff[0]), mod0, w1, w2, 0, tf)

    mod_kv = _ada(c_all, w_ada_kv[None], b_ada_kv[None], 0)
    mod1 = _ada(c_all, w_ada, b_ada, 1)
    w_kv2 = bf(w_kv.reshape(d, 2 * SWA_W))
    w_q = bf(w_swa_q[0].reshape(d, SWA_W))
    w_so = bf(w_swa_o[0].reshape(SWA_W, d))
    ngrp = len(SWA_GROUPS)

    kv_p = _shared_kv(prm, xs[prm], row(g_kv_norm), mod_kv, w_kv2, row(g_swa_k), rope_swa[prm], True)
    q_p = _swa_q(prm, xs[prm], row(g_norm_mix[1]), mod1, w_q, row(g_swa_q[0]), rope_swa[prm], True)
    k_p, v_p, kd, vd, qd = kv_p[0], kv_p[1], kv_p[2:2 + ngrp], kv_p[2 + ngrp:], q_p[1:]
    groups = [_swa_prompt_group(qd[g], kd[g], vd[g], g) for g in range(ngrp)]
    xs[prm] = _swa_proj_res(prm, [o for o, _ in groups], [l for _, l in groups], w_so, xs[prm], mod1, 2)

    k_s, v_s = _shared_kv(smp, xs[smp], row(g_kv_norm), mod_kv, w_kv2, row(g_swa_k), rope_swa[smp], False)
    (q_s,) = _swa_q(smp, xs[smp], row(g_norm_mix[1]), mod1, w_q, row(g_swa_q[0]), rope_swa[smp], False)
    head_rows = lambda x: jnp.pad(x.reshape(db, ngrp, HEADS_PER_GROUP, SWA_HEAD_DIM),
                                  ((0, 0), (0, 0), (0, SUBLANES - HEADS_PER_GROUP), (0, 0)))
    states = ((state_swa_k0, state_swa_v0), (state_swa_k1, state_swa_v1), (state_swa_k2, state_swa_v2))
    o_s = _swa_sample(head_rows(q_s), head_rows(k_s), head_rows(v_s), states)
    o_s = o_s[:, :, :HEADS_PER_GROUP].reshape(1, db, SWA_W).astype(BF16)
    xs[smp] = _proj_res(smp, o_s, w_so, xs[smp], mod1, 2)

    for rows in (prm, smp):
        xs[rows] = _mlp(rows, xs[rows], row(g_norm_ff[1]), mod1, w1, w2, 1, tf)

    def group_heads(x, g, n_rows):
        part = x[:, x.shape[1] - n_rows:, g * GROUP_W:(g + 1) * GROUP_W]
        return part.reshape(x.shape[0], n_rows, HEADS_PER_GROUP, SWA_HEAD_DIM)

    outs = [xs[prm], xs[smp].reshape(db, 1, d),
            lat_p.reshape(nb, t, 1, KV_LORA), kr_p.reshape(nb, t, 1, ROPE_DIM),
            lat_s.reshape(db, 1, 1, KV_LORA), kr_s.reshape(db, 1, 1, ROPE_DIM)]
    for g, (window, _) in enumerate(SWA_GROUPS):
        outs += [group_heads(k_p, g, min(window, t)), group_heads(v_p, g, min(window, t))]
    k_s3, v_s3 = k_s.reshape(db, 1, SWA_W), v_s.reshape(db, 1, SWA_W)
    for g in range(ngrp):
        outs += [group_heads(k_s3, g, 1), group_heads(v_s3, g, 1)]
    return tuple(outs)
```

```python
import functools

import jax
import jax.numpy as jnp
from jax import lax
from jax.experimental import pallas as pl
from jax.experimental.pallas import tpu as pltpu

F32 = jnp.float32
BF16 = jnp.bfloat16

MLA_HEADS = 16
Q_LORA = 512
KV_LORA = 512
NOPE_DIM = 128
ROPE_DIM = 64
V_DIM = 128
SWA_GROUPS = ((128, 1), (512, 4), (2048, 16))
HEADS_PER_GROUP = 4
N_SWA_HEADS = HEADS_PER_GROUP * len(SWA_GROUPS)
SWA_HEAD_DIM = 128
ROT_DIM = SWA_HEAD_DIM // 4
ROPE_THETA = 500000.0
PAST_LEN = 16384
EPS = 1e-6
MLA_SCALE = (NOPE_DIM + ROPE_DIM) ** -0.5
MLA_Q_SCALE = MLA_SCALE * 1.4426950408889634
SWA_SCALE = SWA_HEAD_DIM ** -0.5

LANES = 128
SUBLANES = 8
VMEM_LIMIT_BYTES = 56 * 2**20
SWA_BAND = 128
GROUP_W = HEADS_PER_GROUP * SWA_HEAD_DIM
SWA_W = N_SWA_HEADS * SWA_HEAD_DIM
MLA_W = MLA_HEADS * LANES
MLA_QK = 2 * LANES
FLASH_HEADS = 2
DECODE_PAGES = 32
DECODE_CHUNK = 4
LAT_SPLIT = KV_LORA // LANES
NEG_INF = float("-inf")


def _params(*sem):
    return pltpu.CompilerParams(dimension_semantics=sem, vmem_limit_bytes=VMEM_LIMIT_BYTES)


def _nt_dot(a, b):
    return lax.dot_general(a, b, (((1,), (1,)), ((), ())), preferred_element_type=F32)


def _dot(a, b):
    return jnp.dot(a, b, preferred_element_type=F32)


def _rms(x, g, n):
    ms = jnp.sum(x * x, axis=-1, keepdims=True) * (1.0 / n)
    return x * lax.rsqrt(ms + EPS) * g


def _rope(x, cos, sin_lo, sin_hi, half):
    return x * cos + pltpu.roll(x, LANES - half, 1) * sin_lo + pltpu.roll(x, half, 1) * sin_hi


def _norm_mod(x, g, shift, scale):
    return _rms(x, g, x.shape[-1]) * (1.0 + scale) + shift


def _rope_tables(pos, rot_dim):
    half = rot_dim // 2
    inv_freq = ROPE_THETA ** (-(jnp.arange(half, dtype=F32) * 2.0 / rot_dim))
    ang = pos.astype(F32)[:, None] * inv_freq[None, :]
    cos, sin = jnp.cos(ang), jnp.sin(ang)
    t = pos.shape[0]
    zeros = lambda n: jnp.zeros((t, n), F32)
    cos_t = jnp.concatenate([cos, cos, jnp.ones((t, LANES - rot_dim), F32)], axis=1)
    sin_lo = jnp.concatenate([-sin, zeros(LANES - half)], axis=1)
    sin_hi = jnp.concatenate([zeros(half), sin, zeros(LANES - rot_dim)], axis=1)
    return cos_t, sin_lo, sin_hi


class _Rows:
    def __init__(self, nb, t, tm, mod_view, mod_block, mod_index):
        self.nb, self.t, self.tm = nb, t, tm
        self._mod_view, self._mod_block, self._mod_index = mod_view, mod_block, mod_index

    def mod_arr(self, mod):
        return mod.reshape(self._mod_view(mod.shape[1]))

    def mod_spec(self, k, d):
        idx = self._mod_index
        return pl.BlockSpec(self._mod_block(d), lambda b, *_: idx(b, k))

    def row_spec(self, width):
        return pl.BlockSpec((1, self.tm, width), lambda b, i, *_: (b, i, 0))

    def tab_spec(self):
        return pl.BlockSpec((self.tm, LANES), lambda b, i, *_: (i, 0))


def _full(shape):
    return pl.BlockSpec(shape, lambda *_: (0,) * len(shape))


def _ada_kernel(c_ref, w_ref, b_ref, o_ref):
    c = c_ref[...]
    s = c / (1.0 + jnp.exp(-c))
    o_ref[...] = _dot(s.astype(BF16), w_ref[...].astype(BF16)) + b_ref[...]


def _ada(c, w, b, layer, tn=1024):
    m, d = c.shape
    nl, _, n = w.shape
    return pl.pallas_call(
        _ada_kernel,
        grid=(n // tn,),
        in_specs=[pl.BlockSpec((m, d), lambda j: (0, 0)),
                  pl.BlockSpec((None, d, tn), lambda j: (layer, 0, j)),
                  pl.BlockSpec((None, 1, tn), lambda j: (layer, 0, j))],
        out_specs=pl.BlockSpec((m, tn), lambda j: (0, j)),
        out_shape=jax.ShapeDtypeStruct((m, n), F32),
        compiler_params=_params("arbitrary"),
        name="ada_mod",
    )(c, w, b.reshape(nl, 1, n))


def _mla_in_kernel(x_ref, g_ref, sh_ref, sc_ref, w_ref, gqa_ref, gkva_ref, gkr_ref,
                   cos_ref, slo_ref, shi_ref, cq_ref, lat_ref, latb_ref, kr_ref, krb_ref):
    xn = _norm_mod(x_ref[0], g_ref[...], sh_ref[0], sc_ref[0])
    a = _dot(xn.astype(BF16), w_ref[...])
    cq_ref[0] = _rms(a[:, :Q_LORA], gqa_ref[...], Q_LORA).astype(BF16)
    lat = _rms(a[:, Q_LORA:Q_LORA + KV_LORA], gkva_ref[...], KV_LORA)
    lat_ref[0] = lat
    latb_ref[0] = lat.astype(BF16)
    kr = _rms(a[:, Q_LORA + KV_LORA:], gkr_ref[...], ROPE_DIM)
    kr = _rope(kr, cos_ref[...], slo_ref[...], shi_ref[...], ROPE_DIM // 2)
    kr_ref[0] = kr[:, :ROPE_DIM]
    krb_ref[0] = kr.astype(BF16)


def _mla_in(rows, x, g, mod, w, gqa, gkva, gkr, rope):
    d = x.shape[-1]
    n = w.shape[1]
    marr = rows.mod_arr(mod)
    shape = lambda width, dt: jax.ShapeDtypeStruct((rows.nb, rows.t, width), dt)
    return pl.pallas_call(
        _mla_in_kernel,
        grid=(rows.nb, rows.t // rows.tm),
        in_specs=[rows.row_spec(d), _full((1, d)), rows.mod_spec(0, d), rows.mod_spec(1, d),
                  _full((d, n)), _full((1, Q_LORA)), _full((1, KV_LORA)), _full((1, LANES)),
                  rows.tab_spec(), rows.tab_spec(), rows.tab_spec()],
        out_specs=[rows.row_spec(Q_LORA), rows.row_spec(KV_LORA), rows.row_spec(KV_LORA),
                   rows.row_spec(ROPE_DIM), rows.row_spec(LANES)],
        out_shape=[shape(Q_LORA, BF16), shape(KV_LORA, F32), shape(KV_LORA, BF16),
                   shape(ROPE_DIM, F32), shape(LANES, BF16)],
        compiler_params=_params("arbitrary", "arbitrary"),
        name="mla_in",
    )(x, g, marr, marr, w, gqa, gkva, gkr, *rope)


def _q_up_kernel(cq_ref, wn_ref, wr_ref, gqn_ref, gqr_ref, cos_ref, slo_ref, shi_ref, q_ref):
    cq = cq_ref[0]
    an = _dot(cq, wn_ref[...])
    ar = _dot(cq, wr_ref[...])
    cos, slo, shi = cos_ref[...], slo_ref[...], shi_ref[...]
    for h in range(MLA_HEADS):
        sl = slice(h * LANES, (h + 1) * LANES)
        qn = _rms(an[:, sl], gqn_ref[...], NOPE_DIM) * MLA_Q_SCALE
        qr = _rope(_rms(ar[:, sl], gqr_ref[...], ROPE_DIM), cos, slo, shi, ROPE_DIM // 2) * MLA_Q_SCALE
        q_ref[0, h, :, 0:LANES] = qn.astype(BF16)
        q_ref[0, h, :, LANES:MLA_QK] = qr.astype(BF16)


def _q_up(rows, cq, wn, wr, gqn, gqr, rope):
    tm = rows.tm
    return pl.pallas_call(
        _q_up_kernel,
        grid=(rows.nb, rows.t // tm),
        in_specs=[rows.row_spec(Q_LORA), _full((Q_LORA, MLA_W)), _full((Q_LORA, MLA_W)),
                  _full((1, LANES)), _full((1, LANES)),
                  rows.tab_spec(), rows.tab_spec(), rows.tab_spec()],
        out_specs=pl.BlockSpec((1, MLA_HEADS, tm, MLA_QK), lambda b, i: (b, 0, i, 0)),
        out_shape=jax.ShapeDtypeStruct((rows.nb, MLA_HEADS, rows.t, MLA_QK), BF16),
        compiler_params=_params("arbitrary", "arbitrary"),
        name="mla_q_up",
    )(cq, wn, wr, gqn, gqr, *rope)


def _kv_up_kernel(transposed, latb_ref, krb_ref, wk_ref, wv_ref, gkn_ref, k_ref, v_ref):
    lat = latb_ref[0]
    ak = _dot(lat, wk_ref[...])
    av = _dot(lat, wv_ref[...])
    krb = krb_ref[0]
    if transposed:
        kr_t = krb.astype(F32).T.astype(BF16)
    for h in range(MLA_HEADS):
        sl = slice(h * LANES, (h + 1) * LANES)
        kn = _rms(ak[:, sl], gkn_ref[...], NOPE_DIM)
        if transposed:
            k_ref[0, h, 0:LANES, :] = kn.T.astype(BF16)
            k_ref[0, h, LANES:MLA_QK, :] = kr_t
        else:
            k_ref[0, h, :, 0:LANES] = kn.astype(BF16)
            k_ref[0, h, :, LANES:MLA_QK] = krb
        v_ref[0, h] = av[:, sl].astype(BF16)


def _kv_up(rows, latb, krb, wk, wv, gkn, transposed):
    tm = rows.tm
    hspec = lambda w: pl.BlockSpec((1, MLA_HEADS, tm, w), lambda b, i: (b, 0, i, 0))
    if transposed:
        kspec = pl.BlockSpec((1, MLA_HEADS, MLA_QK, tm), lambda b, i: (b, 0, 0, i))
        kshape = (rows.nb, MLA_HEADS, MLA_QK, rows.t)
    else:
        kspec, kshape = hspec(MLA_QK), (rows.nb, MLA_HEADS, rows.t, MLA_QK)
    return pl.pallas_call(
        functools.partial(_kv_up_kernel, transposed),
        grid=(rows.nb, rows.t // tm),
        in_specs=[rows.row_spec(KV_LORA), rows.row_spec(LANES), _full((KV_LORA, MLA_W)),
                  _full((KV_LORA, MLA_W)), _full((1, LANES))],
        out_specs=[kspec, hspec(V_DIM)],
        out_shape=[jax.ShapeDtypeStruct(kshape, BF16),
                   jax.ShapeDtypeStruct((rows.nb, MLA_HEADS, rows.t, V_DIM), BF16)],
        compiler_params=_params("arbitrary", "arbitrary"),
        name="mla_kv_up",
    )(latb, krb, wk, wv, gkn)


def _flash_kernel(qi_ref, ki_ref, q_ref, k_ref, v_ref, o_ref, m_ref, l_ref, acc_ref):
    s_idx = pl.program_id(2)
    qi, ki = qi_ref[s_idx], ki_ref[s_idx]

    @pl.when(ki == 0)
    def _():
        m_ref[...] = jnp.full(m_ref.shape, NEG_INF, F32)
        l_ref[...] = jnp.zeros(l_ref.shape, F32)
        acc_ref[...] = jnp.zeros(acc_ref.shape, F32)

    def update(diagonal):
        for h in range(FLASH_HEADS):
            s = _dot(q_ref[0, h], k_ref[0, h])
            if diagonal:
                rows = lax.broadcasted_iota(jnp.int32, s.shape, 0)
                cols = lax.broadcasted_iota(jnp.int32, s.shape, 1)
                s = jnp.where(cols <= rows, s, NEG_INF)
            m_prev = m_ref[h]
            m_new = jnp.maximum(m_prev, jnp.max(s, axis=-1, keepdims=True))
            corr = jnp.exp2(m_prev - m_new)
            p = jnp.exp2(s - m_new)
            l_ref[h] = corr * l_ref[h] + jnp.sum(p, axis=-1, keepdims=True)
            acc_ref[h] = corr * acc_ref[h] + _dot(p.astype(BF16), v_ref[0, h])
            m_ref[h] = m_new

    @pl.when(ki < qi)
    def _():
        update(False)

    @pl.when(ki == qi)
    def _():
        update(True)
        for h in range(FLASH_HEADS):
            o_ref[0, :, h * V_DIM:(h + 1) * V_DIM] = (acc_ref[h] / l_ref[h]).astype(o_ref.dtype)


def _flash(q, k, v, tq):
    nb, nh, t, _ = q.shape
    nq = t // tq
    fh = FLASH_HEADS
    pairs = [(i, j) for i in range(nq) for j in range(i + 1)]
    qi = jnp.asarray([p[0] for p in pairs], jnp.int32)
    ki = jnp.asarray([p[1] for p in pairs], jnp.int32)
    grid_spec = pltpu.PrefetchScalarGridSpec(
        num_scalar_prefetch=2,
        grid=(nb, nh // fh, len(pairs)),
        in_specs=[pl.BlockSpec((1, fh, tq, MLA_QK), lambda b, h, s, qi, ki: (b, h, qi[s], 0)),
                  pl.BlockSpec((1, fh, MLA_QK, tq), lambda b, h, s, qi, ki: (b, h, 0, ki[s])),
                  pl.BlockSpec((1, fh, tq, V_DIM), lambda b, h, s, qi, ki: (b, h, ki[s], 0))],
        out_specs=pl.BlockSpec((1, tq, fh * V_DIM), lambda b, h, s, qi, ki: (b, qi[s], h)),
        scratch_shapes=[pltpu.VMEM((fh, tq, 1), F32), pltpu.VMEM((fh, tq, 1), F32),
                        pltpu.VMEM((fh, tq, V_DIM), F32)],
    )
    return pl.pallas_call(
        _flash_kernel,
        grid_spec=grid_spec,
        out_shape=jax.ShapeDtypeStruct((nb, t, nh * V_DIM), BF16),
        compiler_params=_params("arbitrary", "arbitrary", "arbitrary"),
        name="mla_flash",
    )(qi, ki, q, k, v)


def _proj_res_kernel(a_ref, w_ref, x_ref, gate_ref, o_ref):
    o_ref[0] = x_ref[0] + gate_ref[0] * _dot(a_ref[0], w_ref[...])


def _proj_res(rows, a, w, x, mod, gate_k):
    kdim, d = w.shape
    return pl.pallas_call(
        _proj_res_kernel,
        grid=(rows.nb, rows.t // rows.tm),
        in_specs=[rows.row_spec(kdim), _full((kdim, d)), rows.row_spec(d), rows.mod_spec(gate_k, d)],
        out_specs=rows.row_spec(d),
        out_shape=jax.ShapeDtypeStruct((rows.nb, rows.t, d), F32),
        compiler_params=_params("arbitrary", "arbitrary"),
        name="proj_residual",
    )(a, w, x, rows.mod_arr(mod))


def _dilated_rows(r, tm, dil):
    return pl.ds(r, tm // dil, stride=dil) if dil > 1 else pl.ds(0, tm)


def _dilated_spec(tm, dil):
    return pl.BlockSpec((1, dil, tm // dil, GROUP_W), lambda b, i, *_: (b, 0, i, 0))


def _swa_proj_res_kernel(o0_ref, o1_ref, o2_ref, l0_ref, l1_ref, l2_ref, w_ref, x_ref, gate_ref, out_ref,
                         a_scr, lse_scr):
    tm = a_scr.shape[1]
    for g, (o_ref, l_ref) in enumerate(((o0_ref, l0_ref), (o1_ref, l1_ref), (o2_ref, l2_ref))):
        dil = SWA_GROUPS[g][1]
        for r in range(dil):
            for j in range(HEADS_PER_GROUP):
                sl = slice(j * SWA_HEAD_DIM, (j + 1) * SWA_HEAD_DIM)
                a_scr[g * HEADS_PER_GROUP + j, _dilated_rows(r, tm, dil), :] = o_ref[0, r][:, sl].astype(F32)
                lse_scr[g * HEADS_PER_GROUP + j, _dilated_rows(r, tm, dil), :] = l_ref[0, r][:, sl]
    mixed = [None] * N_SWA_HEADS
    for j in range(HEADS_PER_GROUP):
        heads = [g * HEADS_PER_GROUP + j for g in range(len(SWA_GROUPS))]
        lses = [lse_scr[h] for h in heads]
        mx = jnp.maximum(jnp.maximum(lses[0], lses[1]), lses[2])
        es = [jnp.exp(x - mx) for x in lses]
        inv = 1.0 / (es[0] + es[1] + es[2])
        for h, e in zip(heads, es):
            mixed[h] = (a_scr[h] * (e * inv)).astype(BF16)
    out_ref[0] = x_ref[0] + gate_ref[0] * _dot(jnp.concatenate(mixed, axis=1), w_ref[...])


def _swa_proj_res(rows, os_, lses, w, x, mod, gate_k):
    kdim, d = w.shape
    tm = rows.tm
    dspecs = [_dilated_spec(tm, dil) for _, dil in SWA_GROUPS]
    return pl.pallas_call(
        _swa_proj_res_kernel,
        grid=(rows.nb, rows.t // tm),
        in_specs=dspecs + dspecs + [_full((kdim, d)), rows.row_spec(d), rows.mod_spec(gate_k, d)],
        out_specs=rows.row_spec(d),
        out_shape=jax.ShapeDtypeStruct((rows.nb, rows.t, d), F32),
        scratch_shapes=[pltpu.VMEM((N_SWA_HEADS, tm, SWA_HEAD_DIM), F32)] * 2,
        compiler_params=_params("arbitrary", "arbitrary"),
        name="swa_proj_residual",
    )(*os_, *lses, w, x, rows.mod_arr(mod))


def _mlp_kernel(x_ref, g_ref, sh_ref, sc_ref, gate_ref, w1_ref, w2_ref, o_ref, xn_ref, acc_ref):
    f = pl.program_id(2)

    @pl.when(f == 0)
    def _():
        xn_ref[...] = _norm_mod(x_ref[0], g_ref[...], sh_ref[0], sc_ref[0]).astype(BF16)
        acc_ref[...] = jnp.zeros(acc_ref.shape, F32)

    h = jnp.maximum(_dot(xn_ref[...], w1_ref[...]), 0.0)
    acc_ref[...] += _dot((h * h).astype(BF16), w2_ref[...])

    @pl.when(f == pl.num_programs(2) - 1)
    def _():
        o_ref[0] = x_ref[0] + gate_ref[0] * acc_ref[...]


def _mlp(rows, x, g, mod, w1, w2, layer, tf):
    _, d, dff = w1.shape
    tm = rows.tm
    marr = rows.mod_arr(mod)
    return pl.pallas_call(
        _mlp_kernel,
        grid=(rows.nb, rows.t // tm, dff // tf),
        in_specs=[rows.row_spec(d), _full((1, d)), rows.mod_spec(3, d), rows.mod_spec(4, d),
                  rows.mod_spec(5, d),
                  pl.BlockSpec((None, d, tf), lambda b, i, f: (layer, 0, f)),
                  pl.BlockSpec((None, tf, d), lambda b, i, f: (layer, f, 0))],
        out_specs=rows.row_spec(d),
        out_shape=jax.ShapeDtypeStruct((rows.nb, rows.t, d), F32),
        scratch_shapes=[pltpu.VMEM((tm, d), BF16), pltpu.VMEM((tm, d), F32)],
        compiler_params=_params("arbitrary", "arbitrary", "arbitrary"),
        name="mlp",
    )(x, g, marr, marr, marr, w1, w2)


def _decode_kernel(pp, chunk, pt_ref, *refs):
    lat_refs, kr_refs = refs[:pp], refs[pp:2 * pp]
    (wukt_ref, wq_ref, q_ref, kc_ref, latn_ref, o_ref, m_ref, l_ref, acc_ref) = refs[2 * pp:]
    j = pl.program_id(1)

    @pl.when(j == 0)
    def _():
        m_ref[...] = jnp.full(m_ref.shape, NEG_INF, F32)
        l_ref[...] = jnp.zeros(l_ref.shape, F32)
        acc_ref[...] = jnp.zeros(acc_ref.shape, F32)

    def page(ref):
        n = ref.shape[0] // LAT_SPLIT
        return jnp.concatenate([ref[pl.ds(c, n, stride=LAT_SPLIT), :] for c in range(LAT_SPLIT)], axis=1)

    wq = wq_ref[0]
    qr = q_ref[0][:, LANES:LANES + ROPE_DIM].astype(BF16)
    lats, raws, scores = [], [], []

    def finish(c):
        raw3 = raws[c].reshape(MLA_HEADS, NOPE_DIM, raws[c].shape[-1])
        ssq = jnp.sum(raw3 * raw3, axis=1)
        kr_t = jnp.concatenate([kr_refs[c * chunk + i][0] for i in range(chunk)], axis=1).astype(BF16)
        scores.append(_nt_dot(wq, lats[c]) * lax.rsqrt(ssq * (1.0 / NOPE_DIM) + EPS) + _dot(qr, kr_t))

    for c in range(pp // chunk):
        lat = jnp.concatenate([page(lat_refs[c * chunk + i]) for i in range(chunk)], axis=0).astype(BF16)
        lats.append(lat)
        raws.append(_nt_dot(wukt_ref[...], lat))
        if c > 0:
            finish(c - 1)
    finish(pp // chunk - 1)
    s = jnp.concatenate(scores, axis=1)
    m_prev = m_ref[...]
    m_new = jnp.maximum(m_prev, jnp.max(s, axis=-1, keepdims=True))
    corr = jnp.exp2(m_prev - m_new)
    p = jnp.exp2(s - m_new)
    l_ref[...] = corr * l_ref[...] + jnp.sum(p, axis=-1, keepdims=True)
    acc_ref[...] = corr * acc_ref[...] + _dot(p.astype(BF16), jnp.concatenate(lats, axis=0))
    m_ref[...] = m_new

    @pl.when(j == pl.num_programs(1) - 1)
    def _():
        s_n = jnp.sum(q_ref[0] * kc_ref[0], axis=-1, keepdims=True)
        m_prev = m_ref[...]
        m_new = jnp.maximum(m_prev, s_n)
        corr = jnp.exp2(m_prev - m_new)
        p_n = jnp.exp2(s_n - m_new)
        l_fin = corr * l_ref[...] + p_n
        acc = corr * acc_ref[...] + p_n * latn_ref[0]
        o_ref[0] = acc / l_fin


def _decode(page_table, cache_lat, cache_kr, wukt, wq, q, kc, latn, pp, chunk):
    db, n_pages = page_table.shape
    psz = cache_kr.shape[2]
    pt = page_table.reshape(-1)
    phys = lambda b, j, pt, i: pt[b * n_pages + j * pp + i]
    lat_spec = lambda i: pl.BlockSpec((psz * LAT_SPLIT, LANES), lambda b, j, pt: (phys(b, j, pt, i), 0))
    kr_spec = lambda i: pl.BlockSpec((1, ROPE_DIM, psz), lambda b, j, pt: (phys(b, j, pt, i), 0, 0))
    per_b = lambda shape: pl.BlockSpec((1,) + shape, lambda b, j, pt: (b, 0, 0))
    grid_spec = pltpu.PrefetchScalarGridSpec(
        num_scalar_prefetch=1,
        grid=(db, n_pages // pp),
        in_specs=([lat_spec(i) for i in range(pp)] + [kr_spec(i) for i in range(pp)]
                  + [pl.BlockSpec(wukt.shape, lambda b, j, pt: (0, 0)),
                     per_b((MLA_HEADS, KV_LORA)), per_b((MLA_HEADS, MLA_QK)), per_b((MLA_HEADS, MLA_QK)),
                     per_b((1, KV_LORA))]),
        out_specs=per_b((MLA_HEADS, KV_LORA)),
        scratch_shapes=[pltpu.VMEM((MLA_HEADS, 1), F32), pltpu.VMEM((MLA_HEADS, 1), F32),
                        pltpu.VMEM((MLA_HEADS, KV_LORA), F32)],
    )
    return pl.pallas_call(
        functools.partial(_decode_kernel, pp, chunk),
        grid_spec=grid_spec,
        out_shape=jax.ShapeDtypeStruct((db, MLA_HEADS, KV_LORA), F32),
        compiler_params=_params("arbitrary", "arbitrary"),
        name="mla_decode",
    )(pt, *([cache_lat] * pp), *([cache_kr] * pp), wukt, wq, q, kc, latn)


def _absorb_kernel(q_ref, gkn_ref, wukt_ref, o_ref):
    qg = (q_ref[0, 0][:, :LANES].astype(F32) * gkn_ref[...]).astype(BF16)
    o_ref[0] = _dot(qg, wukt_ref[0]).astype(BF16)


def _absorb(q, gkn, wukt3):
    _, nh, db, _ = q.shape
    return pl.pallas_call(
        _absorb_kernel,
        grid=(nh,),
        in_specs=[pl.BlockSpec((1, 1, db, MLA_QK), lambda h: (0, h, 0, 0)), _full((1, LANES)),
                  pl.BlockSpec((1, NOPE_DIM, KV_LORA), lambda h: (h, 0, 0))],
        out_specs=pl.BlockSpec((1, db, KV_LORA), lambda h: (h, 0, 0)),
        out_shape=jax.ShapeDtypeStruct((nh, db, KV_LORA), BF16),
        compiler_params=_params("arbitrary"),
        name="mla_absorb_q",
    )(q, gkn, wukt3)


def _uv_kernel(o_ref, w_ref, out_ref):
    out_ref[...] = _dot(o_ref[0].astype(BF16), w_ref[0]).astype(BF16)


def _uv(o_lat, wuv3):
    nh, db, _ = o_lat.shape
    return pl.pallas_call(
        _uv_kernel,
        grid=(nh,),
        in_specs=[pl.BlockSpec((1, db, KV_LORA), lambda h: (h, 0, 0)),
                  pl.BlockSpec((1, KV_LORA, V_DIM), lambda h: (h, 0, 0))],
        out_specs=pl.BlockSpec((db, V_DIM), lambda h: (0, h)),
        out_shape=jax.ShapeDtypeStruct((db, nh * V_DIM), BF16),
        compiler_params=_params("arbitrary"),
        name="mla_uv",
    )(o_lat, wuv3)


def _head_norm_rope(a, g, cos, slo, shi, scale):
    outs = []
    for h in range(a.shape[-1] // LANES):
        y = _rope(_rms(a[:, h * LANES:(h + 1) * LANES], g, SWA_HEAD_DIM), cos, slo, shi, ROT_DIM // 2)
        outs.append(y * scale if scale != 1.0 else y)
    return outs


def _write_dilated(heads, scr_ref, out_refs):
    tm = scr_ref.shape[1]
    for h, y in enumerate(heads):
        scr_ref[h] = y
    for g, (_, dil) in enumerate(SWA_GROUPS):
        for r in range(dil):
            for j in range(HEADS_PER_GROUP):
                rows = scr_ref[g * HEADS_PER_GROUP + j, _dilated_rows(r, tm, dil), :]
                out_refs[g][0, r, :, j * SWA_HEAD_DIM:(j + 1) * SWA_HEAD_DIM] = rows.astype(BF16)


def _dilated_outs(rows):
    specs = [_dilated_spec(rows.tm, dil) for _, dil in SWA_GROUPS]
    shapes = [jax.ShapeDtypeStruct((rows.nb, dil, rows.t // dil, GROUP_W), BF16) for _, dil in SWA_GROUPS]
    return specs, shapes


def _head_scratch(rows, dilated):
    return [pltpu.VMEM((N_SWA_HEADS, rows.tm, SWA_HEAD_DIM), F32)] if dilated else []


def _swa_q_kernel(dilated, x_ref, g_ref, sh_ref, sc_ref, w_ref, gq_ref, cos_ref, slo_ref, shi_ref,
                  q_ref, *rest):
    xn = _norm_mod(x_ref[0], g_ref[...], sh_ref[0], sc_ref[0])
    a = _dot(xn.astype(BF16), w_ref[...])
    heads = _head_norm_rope(a, gq_ref[...], cos_ref[...], slo_ref[...], shi_ref[...], SWA_SCALE)
    q_ref[0] = jnp.concatenate(heads, axis=1)
    if dilated:
        _write_dilated(heads, rest[-1], rest[:-1])


def _swa_q(rows, x, g, mod, w, gq, rope, dilated):
    d, n = w.shape
    marr = rows.mod_arr(mod)
    dspecs, dshapes = _dilated_outs(rows) if dilated else ([], [])
    return pl.pallas_call(
        functools.partial(_swa_q_kernel, dilated),
        grid=(rows.nb, rows.t // rows.tm),
        in_specs=[rows.row_spec(d), _full((1, d)), rows.mod_spec(0, d), rows.mod_spec(1, d),
                  _full((d, n)), _full((1, LANES)), rows.tab_spec(), rows.tab_spec(), rows.tab_spec()],
        out_specs=[rows.row_spec(n)] + dspecs,
        out_shape=[jax.ShapeDtypeStruct((rows.nb, rows.t, n), F32)] + dshapes,
        scratch_shapes=_head_scratch(rows, dilated),
        compiler_params=_params("arbitrary", "arbitrary"),
        name="swa_q",
    )(x, g, marr, marr, w, gq, *rope)


def _shared_kv_kernel(dilated, x_ref, g_ref, sh_ref, sc_ref, w_ref, gk_ref, cos_ref, slo_ref, shi_ref,
                      k_ref, v_ref, *rest):
    xn_ref = rest[2 * len(SWA_GROUPS) if dilated else 0]
    c = pl.program_id(2)
    ngrp = len(SWA_GROUPS)

    @pl.when(c == 0)
    def _():
        xn_ref[...] = _norm_mod(x_ref[0], g_ref[...], sh_ref[0], sc_ref[0]).astype(BF16)
        a = _dot(xn_ref[...], w_ref[...])
        heads = _head_norm_rope(a, gk_ref[...], cos_ref[...], slo_ref[...], shi_ref[...], 1.0)
        k_ref[0] = jnp.concatenate(heads, axis=1)
        if dilated:
            _write_dilated(heads, rest[-1], rest[:ngrp])

    @pl.when(c == 1)
    def _():
        v = _dot(xn_ref[...], w_ref[...])
        v_ref[0] = v
        if dilated:
            heads = [v[:, h * SWA_HEAD_DIM:(h + 1) * SWA_HEAD_DIM] for h in range(N_SWA_HEADS)]
            _write_dilated(heads, rest[-1], rest[ngrp:2 * ngrp])


def _shared_kv(rows, x, g, mod, w, gk, rope, dilated):
    d = x.shape[-1]
    marr = rows.mod_arr(mod)
    dspecs, dshapes = _dilated_outs(rows) if dilated else ([], [])
    return pl.pallas_call(
        functools.partial(_shared_kv_kernel, dilated),
        grid=(rows.nb, rows.t // rows.tm, 2),
        in_specs=[rows.row_spec(d), _full((1, d)), rows.mod_spec(0, d), rows.mod_spec(1, d),
                  pl.BlockSpec((d, SWA_W), lambda b, i, c: (0, c)), _full((1, LANES)),
                  rows.tab_spec(), rows.tab_spec(), rows.tab_spec()],
        out_specs=[rows.row_spec(SWA_W)] * 2 + dspecs + dspecs,
        out_shape=[jax.ShapeDtypeStruct((rows.nb, rows.t, SWA_W), F32)] * 2 + dshapes + dshapes,
        scratch_shapes=[pltpu.VMEM((rows.tm, d), BF16)] + _head_scratch(rows, dilated),
        compiler_params=_params("arbitrary", "arbitrary", "arbitrary"),
        name="shared_kv",
    )(x, g, marr, marr, w, gk, *rope)


def _swa_prompt_kernel(q_ref, kp_ref, kc_ref, vp_ref, vc_ref, o_ref, lse_ref):
    n = pl.program_id(2)
    band = SWA_BAND
    nsub = q_ref.shape[2] // band
    qi = lax.broadcasted_iota(jnp.int32, (band, 2 * band), 0)
    ki = lax.broadcasted_iota(jnp.int32, (band, 2 * band), 1)
    lag = qi + band - ki
    in_band = (lag >= 0) & (lag <= band)
    first_key = jnp.where(n > 0, 0, band)
    for h in range(HEADS_PER_GROUP):
        sl = slice(h * SWA_HEAD_DIM, (h + 1) * SWA_HEAD_DIM)
        keys = jnp.concatenate([kp_ref[0, 0][:, sl], kc_ref[0, 0][:, sl]], axis=0)
        vals = jnp.concatenate([vp_ref[0, 0][:, sl], vc_ref[0, 0][:, sl]], axis=0)
        for i in range(nsub):
            rows = slice(i * band, (i + 1) * band)
            ok = (in_band & (ki >= first_key)) if i == 0 else in_band
            s = _nt_dot(q_ref[0, 0, rows, sl], keys[i * band:(i + 2) * band])
            s = jnp.where(ok, s, NEG_INF)
            m = jnp.max(s, axis=-1, keepdims=True)
            p = jnp.exp(s - m)
            l = jnp.sum(p, axis=-1, keepdims=True)
            o = _dot(p.astype(BF16), vals[i * band:(i + 2) * band]) / l
            o_ref[0, 0, rows, sl] = o.astype(BF16)
            lse_ref[0, 0, rows, sl] = jnp.broadcast_to(m + jnp.log(l), (band, SWA_HEAD_DIM))


def _swa_prompt_group(q, k, v, g, rows_per_step=512):
    nb, dil, m_len, _ = q.shape
    assert SWA_GROUPS[g] == (SWA_BAND * dil, dil) and m_len % SWA_BAND == 0
    rows = min(rows_per_step, m_len)
    nsub = rows // SWA_BAND
    cur = pl.BlockSpec((1, 1, rows, GROUP_W), lambda b, r, n: (b, r, n, 0))
    prev = pl.BlockSpec((1, 1, SWA_BAND, GROUP_W), lambda b, r, n: (b, r, jnp.maximum(n * nsub - 1, 0), 0))
    return pl.pallas_call(
        _swa_prompt_kernel,
        grid=(nb, dil, m_len // rows),
        in_specs=[cur, prev, cur, prev, cur],
        out_specs=[cur, cur],
        out_shape=[jax.ShapeDtypeStruct(q.shape, BF16), jax.ShapeDtypeStruct(q.shape, F32)],
        compiler_params=_params("arbitrary", "arbitrary", "arbitrary"),
        name=f"swa_prompt_g{g}",
    )(q, k, k, v, v)


def _swa_sample_kernel(q_ref, kn_ref, vn_ref, k0_ref, v0_ref, k1_ref, v1_ref, k2_ref, v2_ref, o_ref):
    state = ((k0_ref, v0_ref), (k1_ref, v1_ref), (k2_ref, v2_ref))
    outs, lses = [], []
    for g in range(len(SWA_GROUPS)):
        kb, vb = state[g][0][...], state[g][1][...]
        kb = kb.reshape(-1, SWA_HEAD_DIM).astype(BF16)
        vb = vb.reshape(-1, SWA_HEAD_DIM).astype(BF16)
        slots = kb.shape[0] // SWA_BAND
        q8, kn, vn = q_ref[0, g], kn_ref[0, g], vn_ref[0, g]
        s = _nt_dot(q8.astype(BF16), kb)
        head = lax.broadcasted_iota(jnp.int32, s.shape, 0)
        slot = lax.broadcasted_iota(jnp.int32, s.shape, 1) & (slots - 1)
        s = jnp.where(slot == head, s, NEG_INF)
        s_n = jnp.sum(q8 * kn, axis=-1, keepdims=True)
        m = jnp.maximum(jnp.max(s, axis=-1, keepdims=True), s_n)
        p, p_n = jnp.exp(s - m), jnp.exp(s_n - m)
        l = jnp.sum(p, axis=-1, keepdims=True) + p_n
        outs.append((_dot(p.astype(BF16), vb) + p_n * vn) / l)
        lses.append(m + jnp.log(l))
    mx = jnp.maximum(jnp.maximum(lses[0], lses[1]), lses[2])
    es = [jnp.exp(x - mx) for x in lses]
    inv = 1.0 / (es[0] + es[1] + es[2])
    for g in range(len(SWA_GROUPS)):
        o_ref[0, g] = outs[g] * (es[g] * inv)


def _swa_sample(q, kn, vn, states):
    db = q.shape[0]
    tok = pl.BlockSpec((1, len(SWA_GROUPS), SUBLANES, SWA_HEAD_DIM), lambda b: (b, 0, 0, 0))
    views, specs = [], []
    for g, (window, dil) in enumerate(SWA_GROUPS):
        for buf in states[g]:
            assert buf.shape[1:] == (window, HEADS_PER_GROUP, SWA_HEAD_DIM) and window == SWA_BAND * dil
            if dil == 1:
                views.append(buf.reshape(db * SWA_BAND * HEADS_PER_GROUP, SWA_HEAD_DIM))
                specs.append(pl.BlockSpec((SWA_BAND * HEADS_PER_GROUP, SWA_HEAD_DIM), lambda b: (b, 0)))
            else:
                views.append(buf.reshape(db * SWA_BAND, dil * HEADS_PER_GROUP, SWA_HEAD_DIM))
                specs.append(pl.BlockSpec((SWA_BAND, SUBLANES, SWA_HEAD_DIM), lambda b: (b, 0, 0)))
    return pl.pallas_call(
        _swa_sample_kernel,
        grid=(db,),
        in_specs=[tok, tok, tok] + specs,
        out_specs=tok,
        out_shape=jax.ShapeDtypeStruct(q.shape, F32),
        compiler_params=_params("arbitrary"),
        name="swa_sample",
    )(q, kn, vn, *views)


def kernel(x_prompt, x_sample, c_prompt, c_sample, page_table, cache_mla_latent, cache_mla_krope, state_swa_k0, state_swa_v0, state_swa_k1, state_swa_v1, state_swa_k2, state_swa_v2, g_norm_mix, g_norm_ff, w_ada, b_ada, w_ff1, w_ff2, w_mla_in, g_mla_qa, g_mla_kva, w_mla_uq, w_mla_uk, w_mla_uv, g_mla_qn, g_mla_qr, g_mla_kn, g_mla_kr, w_mla_o, g_kv_norm, w_ada_kv, b_ada_kv, w_kv, g_swa_k, w_swa_q, g_swa_q, w_swa_o):
    nb, t, d = x_prompt.shape
    db, s_len, _ = x_sample.shape
    assert s_len == 1 and w_mla_in.shape[0] == 1 and w_swa_q.shape[0] == 1 and g_norm_mix.shape[0] == 2
    assert db % SUBLANES == 0 and cache_mla_latent.shape[2] == 1
    n_pages = page_table.shape[1]
    n_phys, psz = cache_mla_latent.shape[:2]
    tm = min(512, t)
    tq = min(1024, t)
    chunk = DECODE_CHUNK if n_pages % DECODE_CHUNK == 0 else 1
    pp = DECODE_PAGES if n_pages % DECODE_PAGES == 0 else chunk
    dff = w_ff1.shape[-1]
    tf = min(1024, dff)

    n_mod = -(-(db + nb) // SUBLANES) * SUBLANES
    c_all = jnp.concatenate([c_sample, c_prompt, jnp.zeros((n_mod - db - nb, d), F32)], axis=0)
    prm = _Rows(nb, t, tm, lambda n: (n_mod, 1, n), lambda w: (1, 1, w), lambda b, k: (db + b, 0, k))
    smp = _Rows(1, db, db, lambda n: (1, n_mod, n), lambda w: (1, db, w), lambda b, k: (0, 0, k))

    bf = lambda w: w.astype(BF16)
    pad_lanes = lambda w, n: jnp.pad(w, [(0, 0)] * (w.ndim - 1) + [(0, n - w.shape[-1])])
    row = lambda v: v.reshape(1, -1)

    pos_p = jnp.arange(t, dtype=F32)
    pos_s = jnp.full((db,), PAST_LEN, F32)
    rope_mla = {prm: _rope_tables(pos_p, ROPE_DIM), smp: _rope_tables(pos_s, ROPE_DIM)}
    rope_swa = {prm: _rope_tables(pos_p, ROT_DIM), smp: _rope_tables(pos_s, ROT_DIM)}

    xs = {prm: x_prompt, smp: x_sample.reshape(1, db, d)}

    mod0 = _ada(c_all, w_ada, b_ada, 0)
    w_in = bf(pad_lanes(w_mla_in[0], Q_LORA + KV_LORA + LANES))
    g_kr = pad_lanes(row(g_mla_kr[0]), LANES)
    g_qr = pad_lanes(row(g_mla_qr[0]), LANES)
    w_uq = w_mla_uq[0]
    w_uq_n = bf(w_uq[:, :, :NOPE_DIM].reshape(Q_LORA, MLA_W))
    w_uq_r = bf(pad_lanes(w_uq[:, :, NOPE_DIM:], LANES).reshape(Q_LORA, MLA_W))
    w_uk = bf(w_mla_uk[0].reshape(KV_LORA, MLA_W))
    w_uv = bf(w_mla_uv[0].reshape(KV_LORA, MLA_W))
    w_uk_t3 = bf(jnp.transpose(w_mla_uk[0], (1, 2, 0)))
    w_uv_3 = bf(jnp.transpose(w_mla_uv[0], (1, 0, 2)))
    w_o = bf(w_mla_o[0].reshape(MLA_W, d))
    w1, w2 = bf(w_ff1), bf(w_ff2)

    proj = {}
    for rows in (prm, smp):
        cq, lat, latb, kr, krb = _mla_in(rows, xs[rows], row(g_norm_mix[0]), mod0, w_in, row(g_mla_qa[0]),
                                         row(g_mla_kva[0]), g_kr, rope_mla[rows])
        q = _q_up(rows, cq, w_uq_n, w_uq_r, row(g_mla_qn[0]), g_qr, rope_mla[rows])
        kc, v = _kv_up(rows, latb, krb, w_uk, w_uv, row(g_mla_kn[0]), rows is prm)
        proj[rows] = (q, kc, v, lat, kr)

    q, kc, v, lat_p, kr_p = proj[prm]
    mix_p = _flash(q, kc, v, tq)

    q, kc, _, lat_s, kr_s = proj[smp]
    wq = jnp.transpose(_absorb(q, row(g_mla_kn[0]), w_uk_t3), (1, 0, 2))
    q_b = jnp.transpose(q[0], (1, 0, 2)).astype(F32)
    kc_b = jnp.transpose(kc[0], (1, 0, 2)).astype(F32)
    lat_rows = cache_mla_latent.reshape(n_phys * psz * LAT_SPLIT, LANES)
    kr_t = jnp.transpose(cache_mla_krope, (0, 2, 3, 1)).reshape(n_phys, ROPE_DIM, psz)
    o_lat = _decode(page_table, lat_rows, kr_t, w_uk_t3.reshape(MLA_W, KV_LORA),
                    wq, q_b, kc_b, lat_s.reshape(db, 1, KV_LORA), pp, chunk)
    mix_s = _uv(jnp.transpose(o_lat, (1, 0, 2)), w_uv_3).reshape(1, db, MLA_W)

    mixes = {prm: mix_p, smp: mix_s}
    for rows in (prm, smp):
        x = _proj_res(rows, mixes[rows], w_o, xs[rows], mod0, 2)
        xs[rows] = _mlp(rows, x, row(g_norm_ff[0]), mod0, w1, w2, 0, tf)

    mod_kv = _ada(c_all, w_ada_kv[None], b_ada_kv[None], 0)
    mod1 = _ada(c_all, w_ada, b_ada, 1)
    w_kv2 = bf(w_kv.reshape(d, 2 * SWA_W))
    w_q = bf(w_swa_q[0].reshape(d, SWA_W))
    w_so = bf(w_swa_o[0].reshape(SWA_W, d))
    ngrp = len(SWA_GROUPS)

    kv_p = _shared_kv(prm, xs[prm], row(g_kv_norm), mod_kv, w_kv2, row(g_swa_k), rope_swa[prm], True)
    q_p = _swa_q(prm, xs[prm], row(g_norm_mix[1]), mod1, w_q, row(g_swa_q[0]), rope_swa[prm], True)
    k_p, v_p, kd, vd, qd = kv_p[0], kv_p[1], kv_p[2:2 + ngrp], kv_p[2 + ngrp:], q_p[1:]
    groups = [_swa_prompt_group(qd[g], kd[g], vd[g], g) for g in range(ngrp)]
    xs[prm] = _swa_proj_res(prm, [o for o, _ in groups], [l for _, l in groups], w_so, xs[prm], mod1, 2)

    k_s, v_s = _shared_kv(smp, xs[smp], row(g_kv_norm), mod_kv, w_kv2, row(g_swa_k), rope_swa[smp], False)
    (q_s,) = _swa_q(smp, xs[smp], row(g_norm_mix[1]), mod1, w_q, row(g_swa_q[0]), rope_swa[smp], False)
    head_rows = lambda x: jnp.pad(x.reshape(db, ngrp, HEADS_PER_GROUP, SWA_HEAD_DIM),
                                  ((0, 0), (0, 0), (0, SUBLANES - HEADS_PER_GROUP), (0, 0)))
    states = ((state_swa_k0, state_swa_v0), (state_swa_k1, state_swa_v1), (state_swa_k2, state_swa_v2))
    o_s = _swa_sample(head_rows(q_s), head_rows(k_s), head_rows(v_s), states)
    o_s = o_s[:, :, :HEADS_PER_GROUP].reshape(1, db, SWA_W).astype(BF16)
    xs[smp] = _proj_res(smp, o_s, w_so, xs[smp], mod1, 2)

    for rows in (prm, smp):
        xs[rows] = _mlp(rows, xs[rows], row(g_norm_ff[1]), mod1, w1, w2, 1, tf)

    def group_heads(x, g, n_rows):
        part = x[:, x.shape[1] - n_rows:, g * GROUP_W:(g + 1) * GROUP_W]
        return part.reshape(x.shape[0], n_rows, HEADS_PER_GROUP, SWA_HEAD_DIM)

    outs = [xs[prm], xs[smp].reshape(db, 1, d),
            lat_p.reshape(nb, t, 1, KV_LORA), kr_p.reshape(nb, t, 1, ROPE_DIM),
            lat_s.reshape(db, 1, 1, KV_LORA), kr_s.reshape(db, 1, 1, ROPE_DIM)]
    for g, (window, _) in enumerate(SWA_GROUPS):
        outs += [group_heads(k_p, g, min(window, t)), group_heads(v_p, g, min(window, t))]
    k_s3, v_s3 = k_s.reshape(db, 1, SWA_W), v_s.reshape(db, 1, SWA_W)
    for g in range(ngrp):
        outs += [group_heads(k_s3, g, 1), group_heads(v_s3, g, 1)]
    return tuple(outs)
```
